```python
import jax, jax.numpy as jnp
from jax import lax
import numpy as np

D_MODEL = 2048
BATCH = 1
SEQ = 8192
DEPTH = 1
DEC_BATCH = 16
DEC_SEQ = 16
PAST_LEN = 4096

CHUNK = 64
Q_BLOCK = 128
PLE_DIM = 256
H_A = 16
DK_A = 128
DV_A = D_MODEL // H_A
H_B = 16
DH_B = 128
D_FF = 5632
W_AK = H_A * DK_A
W_AV = H_A * DV_A
W_B = H_B * DH_B
IN_SIZES = (W_AK, W_AK, W_AV, W_AV, W_B, W_B, W_B, H_B, D_MODEL, D_MODEL)
N_IN = sum(IN_SIZES)
EPS = 1e-6
FOX_SCALE = DH_B ** -0.5

kernel_name = "hgrn2_fox_gated_macaron_stream_step"


def _rmsnorm(x, w):
    xf = x.astype(jnp.float32)
    r = lax.rsqrt(jnp.mean(xf * xf, axis=-1, keepdims=True) + EPS)
    return (xf * r).astype(x.dtype) * w


def _swiglu(x, wg, wu, wd):
    return (jax.nn.silu(x @ wg) * (x @ wu)) @ wd


def _hgrn2_chunk(S, inp):
    q, k, v, g = [a.astype(jnp.float32) for a in inp]
    C = q.shape[1]
    b = jnp.cumsum(g, axis=1)
    causal = jnp.tril(jnp.ones((C, C), dtype=bool))
    diff = b[:, :, None] - b[:, None, :]
    decay = jnp.exp(jnp.where(causal[None, :, :, None, None], diff, -jnp.inf))
    A = jnp.einsum('bthd,btshd,bshd->bths', q, decay, k)
    o = jnp.einsum('bths,bshv->bthv', A, v) + jnp.einsum('bthd,bhdv->bthv', q * jnp.exp(b), S)
    b_last = b[:, -1]
    k_dec = k * jnp.exp(b_last[:, None] - b)
    S_new = jnp.exp(b_last)[..., None] * S + jnp.einsum('bshd,bshv->bhdv', k_dec, v)
    return S_new, o


def _hgrn2_mix(q, k, v, logf, S0, chunk):
    B, T = q.shape[0], q.shape[1]
    n = T // chunk
    def split(a):
        return a.reshape((B, n, chunk) + a.shape[2:]).swapaxes(0, 1)
    S_fin, o = lax.scan(_hgrn2_chunk, S0, (split(q), split(k), split(v), split(logf)))
    o = o.swapaxes(0, 1).reshape((B, T) + o.shape[3:])
    return o, S_fin


def _fox_attend(q, cq, qpos, k, v, ck, kpos):
    s = jnp.einsum('bqhd,bkhd->bhqk', q, k).astype(jnp.float32) * FOX_SCALE
    s = s + jnp.swapaxes(cq, 1, 2)[..., :, None] - jnp.swapaxes(ck, 1, 2)[..., None, :]
    s = jnp.where(kpos[None, :] <= qpos[:, None], s, -jnp.inf)
    p = jax.nn.softmax(s, axis=-1)
    return jnp.einsum('bhqk,bkhd->bqhd', p.astype(v.dtype), v)


def _fox_prompt(q, k, v, logf):
    B, S = q.shape[0], q.shape[1]
    c = jnp.cumsum(logf, axis=1)
    nb = S // Q_BLOCK
    qb = q.reshape(B, nb, Q_BLOCK, H_B, DH_B).swapaxes(0, 1)
    cb = c.reshape(B, nb, Q_BLOCK, H_B).swapaxes(0, 1)
    kpos = jnp.arange(S)
    def block(args):
        qi, ci, q0 = args
        return _fox_attend(qi, ci, q0 + jnp.arange(Q_BLOCK), k, v, c, kpos)
    o = lax.map(block, (qb, cb, jnp.arange(nb) * Q_BLOCK))
    return o.swapaxes(0, 1).reshape(B, S, H_B, DH_B)


def _mixer_inputs(u, w_in, fox_fbias, lb):
    z = u @ w_in
    offs = [int(o) for o in np.cumsum(IN_SIZES)[:-1]]
    qa, fa, ia, oga, qb, kb, vb, fl, ga, gb = jnp.split(z, offs, axis=-1)
    B, T = u.shape[0], u.shape[1]
    qa = jax.nn.silu(qa).reshape(B, T, H_A, DK_A)
    f = lb + (1.0 - lb) * jax.nn.sigmoid(fa.astype(jnp.float32))
    ka = (1.0 - f).reshape(B, T, H_A, DK_A)
    logfa = jnp.log(f).reshape(B, T, H_A, DK_A)
    ia = ia.reshape(B, T, H_A, DV_A)
    qb = qb.reshape(B, T, H_B, DH_B)
    kb = kb.reshape(B, T, H_B, DH_B)
    vb = vb.reshape(B, T, H_B, DH_B)
    logfb = jax.nn.log_sigmoid((fl + fox_fbias).astype(jnp.float32))
    return qa, ka, ia, logfa, oga, qb, kb, vb, logfb, ga, gb


def _mixer_outputs(oa, oga, ob, ga, gb, gnorm, w_ba, w_bb, w_out, dtype):
    B, T = oa.shape[0], oa.shape[1]
    oa = oa * lax.rsqrt(jnp.mean(oa * oa, axis=-1, keepdims=True) + EPS)
    oa = (oa.astype(dtype) * gnorm).reshape(B, T, W_AV) * jax.nn.silu(oga)
    ya = oa @ w_ba
    yb = ob.reshape(B, T, W_B) @ w_bb
    m = jax.nn.sigmoid(ga) * ya + jax.nn.sigmoid(gb) * yb
    return m @ w_out


def _layer(h, pe, l, mixer, W):
    h = h + 0.5 * _rmsnorm(_swiglu(_rmsnorm(h, W['ffn1_pre'][l]), W['ffn1_wg'][l], W['ffn1_wu'][l], W['ffn1_wd'][l]), W['ffn1_post'][l])
    u = _rmsnorm(h, W['mix_pre'][l])
    lb = jnp.cumsum(jax.nn.softmax(W['hgrn_lb_logits'].astype(jnp.float32), axis=0), axis=0)[l]
    qa, ka, ia, logfa, oga, qb, kb, vb, logfb, ga, gb = _mixer_inputs(u, W['w_in'][l], W['fox_fbias'][l], lb)
    oa, ob, st = mixer(qa, ka, ia, logfa, qb, kb, vb, logfb, l)
    y = _mixer_outputs(oa, oga, ob, ga, gb, W['hgrn_gnorm'][l], W['w_branch_a'][l], W['w_branch_b'][l], W['w_out'][l], h.dtype)
    h = h + _rmsnorm(y, W['mix_post'][l])
    h = h + 0.5 * _rmsnorm(_swiglu(_rmsnorm(h, W['ffn2_pre'][l]), W['ffn2_wg'][l], W['ffn2_wu'][l], W['ffn2_wd'][l]), W['ffn2_post'][l])
    gate = jax.nn.sigmoid(_rmsnorm(h, W['ple_pre'][l]) @ W['w_ple_gate'][l])
    h = h + _rmsnorm(gate * (pe @ W['w_ple_proj'][l]), W['ple_post'][l])
    return h, st


def setup_inputs(seed: int = 0) -> dict:
    key = jax.random.key(seed)
    ks = iter(jax.random.split(key, 48))
    def nrm(shape, scale):
        return jax.random.normal(next(ks), shape, jnp.float32) * scale
    def gain(shape):
        return 1.0 + nrm(shape, 0.05)
    d = {}
    d['x_prompt'] = nrm((BATCH, SEQ, D_MODEL), 1.0)
    d['x_sample'] = nrm((DEC_BATCH, DEC_SEQ, D_MODEL), 1.0)
    d['cache_fox_k'] = nrm((DEPTH, DEC_BATCH, PAST_LEN, H_B, DH_B), 1.0)
    d['cache_fox_v'] = nrm((DEPTH, DEC_BATCH, PAST_LEN, H_B, DH_B), 1.0)
    d['cache_fox_logf'] = jax.nn.log_sigmoid(1.0 + nrm((DEPTH, DEC_BATCH, PAST_LEN, H_B), 1.0))
    d['state_hgrn'] = nrm((DEPTH, DEC_BATCH, H_A, DK_A, DV_A), 0.5)
    d['p_prompt'] = nrm((DEPTH, BATCH, SEQ, PLE_DIM), 1.0)
    d['p_sample'] = nrm((DEPTH, DEC_BATCH, DEC_SEQ, PLE_DIM), 1.0)
    d['ffn1_pre'] = gain((DEPTH, D_MODEL))
    d['ffn1_post'] = gain((DEPTH, D_MODEL))
    d['ffn1_wg'] = nrm((DEPTH, D_MODEL, D_FF), D_MODEL ** -0.5)
    d['ffn1_wu'] = nrm((DEPTH, D_MODEL, D_FF), D_MODEL ** -0.5)
    d['ffn1_wd'] = nrm((DEPTH, D_FF, D_MODEL), D_FF ** -0.5)
    d['mix_pre'] = gain((DEPTH, D_MODEL))
    d['mix_post'] = gain((DEPTH, D_MODEL))
    d['w_in'] = nrm((DEPTH, D_MODEL, N_IN), D_MODEL ** -0.5)
    d['fox_fbias'] = 1.0 + nrm((DEPTH, H_B), 0.1)
    d['hgrn_lb_logits'] = nrm((DEPTH + 1, W_AK), 0.5)
    d['hgrn_gnorm'] = gain((DEPTH, DV_A))
    d['w_branch_a'] = nrm((DEPTH, W_AV, D_MODEL), W_AV ** -0.5)
    d['w_branch_b'] = nrm((DEPTH, W_B, D_MODEL), W_B ** -0.5)
    d['w_out'] = nrm((DEPTH, D_MODEL, D_MODEL), D_MODEL ** -0.5)
    d['ffn2_pre'] = gain((DEPTH, D_MODEL))
    d['ffn2_post'] = gain((DEPTH, D_MODEL))
    d['ffn2_wg'] = nrm((DEPTH, D_MODEL, D_FF), D_MODEL ** -0.5)
    d['ffn2_wu'] = nrm((DEPTH, D_MODEL, D_FF), D_MODEL ** -0.5)
    d['ffn2_wd'] = nrm((DEPTH, D_FF, D_MODEL), D_FF ** -0.5)
    d['ple_pre'] = gain((DEPTH, D_MODEL))
    d['ple_post'] = gain((DEPTH, D_MODEL))
    d['w_ple_gate'] = nrm((DEPTH, D_MODEL, D_MODEL), D_MODEL ** -0.5)
    d['w_ple_proj'] = nrm((DEPTH, PLE_DIM, D_MODEL), PLE_DIM ** -0.5)
    return d


def reference(x_prompt, x_sample, cache_fox_k, cache_fox_v, cache_fox_logf, state_hgrn, p_prompt, p_sample,
              ffn1_pre, ffn1_post, ffn1_wg, ffn1_wu, ffn1_wd, mix_pre, mix_post, w_in, fox_fbias,
              hgrn_lb_logits, hgrn_gnorm, w_branch_a, w_branch_b, w_out, ffn2_pre, ffn2_post,
              ffn2_wg, ffn2_wu, ffn2_wd, ple_pre, ple_post, w_ple_gate, w_ple_proj):
    W = dict(ffn1_pre=ffn1_pre, ffn1_post=ffn1_post, ffn1_wg=ffn1_wg, ffn1_wu=ffn1_wu, ffn1_wd=ffn1_wd,
             mix_pre=mix_pre, mix_post=mix_post, w_in=w_in, fox_fbias=fox_fbias, hgrn_lb_logits=hgrn_lb_logits,
             hgrn_gnorm=hgrn_gnorm, w_branch_a=w_branch_a, w_branch_b=w_branch_b, w_out=w_out,
             ffn2_pre=ffn2_pre, ffn2_post=ffn2_post, ffn2_wg=ffn2_wg, ffn2_wu=ffn2_wu, ffn2_wd=ffn2_wd,
             ple_pre=ple_pre, ple_post=ple_post, w_ple_gate=w_ple_gate, w_ple_proj=w_ple_proj)

    def prompt_mixer(qa, ka, ia, logfa, qb, kb, vb, logfb, l):
        S0 = jnp.zeros((qa.shape[0], H_A, DK_A, DV_A), jnp.float32)
        oa, S = _hgrn2_mix(qa, ka, ia, logfa, S0, CHUNK)
        ob = _fox_prompt(qb, kb, vb, logfb)
        return oa, ob, (kb, vb, logfb, S)

    def sample_mixer(qa, ka, ia, logfa, qb, kb, vb, logfb, l):
        T = qa.shape[1]
        oa, S = _hgrn2_mix(qa, ka, ia, logfa, state_hgrn[l].astype(jnp.float32), T)
        past = cache_fox_k.shape[2]
        k_all = jnp.concatenate([cache_fox_k[l].astype(kb.dtype), kb], axis=1)
        v_all = jnp.concatenate([cache_fox_v[l].astype(vb.dtype), vb], axis=1)
        c_all = jnp.cumsum(jnp.concatenate([cache_fox_logf[l].astype(jnp.float32), logfb], axis=1), axis=1)
        ob = _fox_attend(qb, c_all[:, past:], past + jnp.arange(T), k_all, v_all, c_all, jnp.arange(past + T))
        return oa, ob, (kb, vb, logfb, S)

    hp, hs = x_prompt, x_sample
    sp, ss = [], []
    for l in range(DEPTH):
        hp, stp = _layer(hp, p_prompt[l], l, prompt_mixer, W)
        hs, sts = _layer(hs, p_sample[l], l, sample_mixer, W)
        sp.append(stp)
        ss.append(sts)
    dt_p, dt_s = x_prompt.dtype, x_sample.dtype
    new_k_prompt = jnp.stack([s[0] for s in sp]).astype(dt_p)
    new_v_prompt = jnp.stack([s[1] for s in sp]).astype(dt_p)
    new_logf_prompt = jnp.stack([s[2] for s in sp]).astype(dt_p)
    new_hgrn_prompt = jnp.stack([s[3] for s in sp]).astype(dt_p)
    new_k_sample = jnp.stack([s[0] for s in ss]).astype(dt_s)
    new_v_sample = jnp.stack([s[1] for s in ss]).astype(dt_s)
    new_logf_sample = jnp.stack([s[2] for s in ss]).astype(dt_s)
    new_hgrn_sample = jnp.stack([s[3] for s in ss]).astype(dt_s)
    return (hp, hs, new_k_prompt, new_v_prompt, new_logf_prompt, new_hgrn_prompt,
            new_k_sample, new_v_sample, new_logf_sample, new_hgrn_sample)
```

```python
import functools

import numpy as np
import jax
import jax.numpy as jnp
from jax import lax
from jax.experimental import pallas as pl
from jax.experimental.pallas import tpu as pltpu

F32 = jnp.float32
BF16 = jnp.bfloat16
EPS = 1e-6
LANES = 128
SUB = 16
VMEM_LIMIT = 56 * 1024 * 1024


def _params(*sem):
    return pltpu.CompilerParams(dimension_semantics=sem, vmem_limit_bytes=VMEM_LIMIT)


def _sigmoid(x):
    return 1.0 / (1.0 + jnp.exp(-x))


def _silu(x):
    return x * _sigmoid(x)


def _log_sigmoid(x):
    return jnp.minimum(x, 0.0) - jnp.log(1.0 + jnp.exp(-jnp.abs(x)))


def _rms(x):
    return x * lax.rsqrt(jnp.mean(x * x, axis=-1, keepdims=True) + EPS)


def _split3(x):
    x1 = x.astype(BF16)
    r1 = x - x1.astype(F32)
    x2 = r1.astype(BF16)
    x3 = (r1 - x2.astype(F32)).astype(BF16)
    return x1, x2, x3


def _dot3(m, parts):
    acc = jnp.dot(m, parts[0], preferred_element_type=F32)
    acc += jnp.dot(m, parts[1], preferred_element_type=F32)
    acc += jnp.dot(m, parts[2], preferred_element_type=F32)
    return acc


_NT = (((1,), (1,)), ((), ()))
_TN = (((0,), (0,)), ((), ()))


def _prenorm_to_scratch(x_ref, nw_ref, xn_ref):
    @pl.when(pl.program_id(1) == 0)
    def _():
        xn_ref[...] = (_rms(x_ref[...]) * nw_ref[...]).astype(BF16)


def _ffn_up_kernel(x_ref, nw_ref, wg_ref, wu_ref, o_ref, xn_ref):
    _prenorm_to_scratch(x_ref, nw_ref, xn_ref)
    xn = xn_ref[...]
    g = jnp.dot(xn, wg_ref[...], preferred_element_type=F32)
    u = jnp.dot(xn, wu_ref[...], preferred_element_type=F32)
    o_ref[...] = (_silu(g) * u).astype(o_ref.dtype)


def _ffn_up(h, nw, wg, wu, tm, tf):
    m, d = h.shape
    f = wg.shape[1]
    return pl.pallas_call(
        _ffn_up_kernel,
        grid=(m // tm, f // tf),
        in_specs=[pl.BlockSpec((tm, d), lambda i, j: (i, 0)),
                  pl.BlockSpec((1, d), lambda i, j: (0, 0)),
                  pl.BlockSpec((d, tf), lambda i, j: (0, j)),
                  pl.BlockSpec((d, tf), lambda i, j: (0, j))],
        out_specs=pl.BlockSpec((tm, tf), lambda i, j: (i, j)),
        out_shape=jax.ShapeDtypeStruct((m, f), BF16),
        scratch_shapes=[pltpu.VMEM((tm, d), BF16)],
        compiler_params=_params("parallel", "arbitrary"),
        name="ffn_up",
    )(h, nw, wg, wu)


def _mm_post_kernel(a_ref, w_ref, h_ref, pw_ref, o_ref, acc_ref, *, scale, nk):
    k = pl.program_id(1)

    @pl.when(k == 0)
    def _():
        acc_ref[...] = jnp.zeros_like(acc_ref)

    acc_ref[...] += jnp.dot(a_ref[...], w_ref[...], preferred_element_type=F32)

    @pl.when(k == nk - 1)
    def _():
        o_ref[...] = h_ref[...] + scale * (_rms(acc_ref[...]) * pw_ref[...])


def _mm_post(a, w, h, pw, scale, tm, tk):
    m, kdim = a.shape
    d = w.shape[1]
    nk = kdim // tk
    return pl.pallas_call(
        functools.partial(_mm_post_kernel, scale=scale, nk=nk),
        grid=(m // tm, nk),
        in_specs=[pl.BlockSpec((tm, tk), lambda i, k: (i, k)),
                  pl.BlockSpec((tk, d), lambda i, k: (k, 0)),
                  pl.BlockSpec((tm, d), lambda i, k: (i, 0)),
                  pl.BlockSpec((1, d), lambda i, k: (0, 0))],
        out_specs=pl.BlockSpec((tm, d), lambda i, k: (i, 0)),
        out_shape=jax.ShapeDtypeStruct((m, d), F32),
        scratch_shapes=[pltpu.VMEM((tm, d), F32)],
        compiler_params=_params("parallel", "arbitrary"),
        name="mm_post",
    )(a, w, h, pw)


def _pn_mm_kernel(x_ref, nw_ref, w_ref, o_ref, xn_ref):
    _prenorm_to_scratch(x_ref, nw_ref, xn_ref)
    o_ref[...] = jnp.dot(xn_ref[...], w_ref[...], preferred_element_type=F32)


def _pn_mm(h, nw, w, tm, tn):
    m, d = h.shape
    n = w.shape[1]
    return pl.pallas_call(
        _pn_mm_kernel,
        grid=(m // tm, n // tn),
        in_specs=[pl.BlockSpec((tm, d), lambda i, j: (i, 0)),
                  pl.BlockSpec((1, d), lambda i, j: (0, 0)),
                  pl.BlockSpec((d, tn), lambda i, j: (0, j))],
        out_specs=pl.BlockSpec((tm, tn), lambda i, j: (i, j)),
        out_shape=jax.ShapeDtypeStruct((m, n), F32),
        scratch_shapes=[pltpu.VMEM((tm, d), BF16)],
        compiler_params=_params("parallel", "arbitrary"),
        name="w_in_proj",
    )(h, nw, w)


def _fox_logf_kernel(x_ref, nw_ref, w_ref, b_ref, o_ref):
    xn = (_rms(x_ref[...]) * nw_ref[...]).astype(BF16)
    z = jnp.dot(xn, w_ref[...], preferred_element_type=F32)
    o_ref[...] = _log_sigmoid(z + b_ref[...])


def _fox_logf(h, nw, w, bias, tm):
    m, d = h.shape
    n = w.shape[1]
    return pl.pallas_call(
        _fox_logf_kernel,
        grid=(m // tm,),
        in_specs=[pl.BlockSpec((tm, d), lambda i: (i, 0)),
                  pl.BlockSpec((1, d), lambda i: (0, 0)),
                  pl.BlockSpec((d, n), lambda i: (0, 0)),
                  pl.BlockSpec((1, n), lambda i: (0, 0))],
        out_specs=pl.BlockSpec((tm, n), lambda i: (i, 0)),
        out_shape=jax.ShapeDtypeStruct((m, n), F32),
        compiler_params=_params("parallel"),
        name="fox_logf",
    )(h, nw, w, bias)


def _cumsum_kernel(x_ref, tri_ref, o_ref, carry_ref):
    @pl.when(pl.program_id(1) == 0)
    def _():
        carry_ref[...] = jnp.zeros_like(carry_ref)

    c = _dot3(tri_ref[...], _split3(x_ref[0])) + carry_ref[...]
    o_ref[0] = c
    carry_ref[...] = c[-1:, :]


def _cumsum_rows(x, tb):
    b, t, n = x.shape
    r = np.arange(tb)
    tri = jnp.asarray(r[None, :] <= r[:, None], BF16)
    return pl.pallas_call(
        _cumsum_kernel,
        grid=(b, t // tb),
        in_specs=[pl.BlockSpec((1, tb, n), lambda i, j: (i, j, 0)),
                  pl.BlockSpec((tb, tb), lambda i, j: (0, 0))],
        out_specs=pl.BlockSpec((1, tb, n), lambda i, j: (i, j, 0)),
        out_shape=jax.ShapeDtypeStruct((b, t, n), F32),
        scratch_shapes=[pltpu.VMEM((1, n), F32)],
        compiler_params=_params("parallel", "arbitrary"),
        name="cumsum_rows",
    )(x, tri)


def _mix_kernel(oa_ref, ob_ref, ga_ref, gb_ref, wa_ref, wb_ref, o_ref):
    ya = jnp.dot(oa_ref[...], wa_ref[...], preferred_element_type=F32)
    yb = jnp.dot(ob_ref[...], wb_ref[...], preferred_element_type=F32)
    o_ref[...] = (_sigmoid(ga_ref[...]) * ya + _sigmoid(gb_ref[...]) * yb).astype(o_ref.dtype)


def _mix(oa, ob, z, ga_off, gb_off, wa, wb, tm, tn):
    m, d = oa.shape
    n = wa.shape[1]
    ga_blk, gb_blk = ga_off // tn, gb_off // tn
    return pl.pallas_call(
        _mix_kernel,
        grid=(m // tm, n // tn),
        in_specs=[pl.BlockSpec((tm, d), lambda i, j: (i, 0)),
                  pl.BlockSpec((tm, d), lambda i, j: (i, 0)),
                  pl.BlockSpec((tm, tn), lambda i, j: (i, ga_blk + j)),
                  pl.BlockSpec((tm, tn), lambda i, j: (i, gb_blk + j)),
                  pl.BlockSpec((d, tn), lambda i, j: (0, j)),
                  pl.BlockSpec((d, tn), lambda i, j: (0, j))],
        out_specs=pl.BlockSpec((tm, tn), lambda i, j: (i, j)),
        out_shape=jax.ShapeDtypeStruct((m, n), BF16),
        compiler_params=_params("parallel", "arbitrary"),
        name="mix_gate",
    )(oa, ob, z, z, wa, wb)


def _ple_kernel(h_ref, nw_ref, wg_ref, pe_ref, wp_ref, pw_ref, o_ref):
    x = h_ref[...]
    xn = (_rms(x) * nw_ref[...]).astype(BF16)
    gate = _sigmoid(jnp.dot(xn, wg_ref[...], preferred_element_type=F32))
    e = jnp.dot(pe_ref[...].astype(BF16), wp_ref[...], preferred_element_type=F32)
    o_ref[...] = x + _rms(gate * e) * pw_ref[...]


def _ple(h, nw, wg, pe, wp, pw, tm):
    m, d = h.shape
    p = pe.shape[1]
    return pl.pallas_call(
        _ple_kernel,
        grid=(m // tm,),
        in_specs=[pl.BlockSpec((tm, d), lambda i: (i, 0)),
                  pl.BlockSpec((1, d), lambda i: (0, 0)),
                  pl.BlockSpec((d, d), lambda i: (0, 0)),
                  pl.BlockSpec((tm, p), lambda i: (i, 0)),
                  pl.BlockSpec((p, d), lambda i: (0, 0)),
                  pl.BlockSpec((1, d), lambda i: (0, 0))],
        out_specs=pl.BlockSpec((tm, d), lambda i: (i, 0)),
        out_shape=jax.ShapeDtypeStruct((m, d), F32),
        compiler_params=_params("parallel"),
        name="ple",
    )(h, nw, wg, pe, wp, pw)


def _hgrn_kernel(*refs, nsb, chain):
    if chain:
        (q_ref, f_ref, v_ref, og_ref, lb_ref, gn_ref, low_ref, upp_ref, sel_ref,
         o_ref, s_out_ref, st_ref) = refs
    else:
        (q_ref, f_ref, v_ref, og_ref, lb_ref, gn_ref, low_ref, upp_ref, sel_ref, s_in_ref,
         o_ref, s_out_ref) = refs
    tb = nsb * SUB
    dk = q_ref.shape[1]

    q = _silu(q_ref[...])
    lb = lb_ref[...]
    f = lb + (1.0 - lb) * _sigmoid(f_ref[...])
    k = 1.0 - f
    g3 = _split3(jnp.log(f))
    low = low_ref[...]
    w = _dot3(low, g3)
    u = _dot3(upp_ref[...], g3)
    qt = (q * jnp.exp(w)).astype(BF16)
    kh = (k * jnp.exp(u)).astype(BF16)
    vb = v_ref[...].astype(BF16)

    q3 = q.reshape(nsb, SUB, dk)
    k3 = k.reshape(nsb, SUB, dk)
    w3 = w.reshape(nsb, SUB, dk)
    pieces = []
    for s in range(SUB):
        dec = jnp.exp(jnp.minimum(w3 - w3[:, s:s + 1, :], 0.0))
        pieces.append((q3 * k3[:, s:s + 1, :] * dec).reshape(tb, dk).astype(BF16))
    a = jnp.dot(jnp.concatenate(pieces, axis=1), sel_ref[...], preferred_element_type=F32)
    a = jnp.where(low > 0, a, 0.0).astype(BF16)
    o_in = jnp.dot(a, vb, preferred_element_type=F32)

    if chain:
        @pl.when(pl.program_id(1) == 0)
        def _():
            st_ref[...] = jnp.zeros_like(st_ref)
        st = st_ref[...]
    outs = []
    for n in range(nsb):
        rows = slice(n * SUB, (n + 1) * SUB)
        if not chain:
            st = s_in_ref[n, 0].T
        outs.append(o_in[rows] + lax.dot_general(qt[rows], st.astype(BF16), _NT,
                                                 preferred_element_type=F32))
        tot = w[(n + 1) * SUB - 1:(n + 1) * SUB, :]
        st = st * jnp.exp(tot) + lax.dot_general(vb[rows], kh[rows], _TN,
                                                 preferred_element_type=F32)
        if not chain:
            s_out_ref[n, 0] = st.T
    if chain:
        st_ref[...] = st

        @pl.when(pl.program_id(1) == pl.num_programs(1) - 1)
        def _():
            s_out_ref[0, 0] = st.T

    o = jnp.concatenate(outs, axis=0)
    o_ref[...] = (_rms(o) * gn_ref[...] * _silu(og_ref[...])).astype(o_ref.dtype)


def _hgrn_consts(tb):
    t = np.arange(tb)
    same = (t[:, None] // SUB) == (t[None, :] // SUB)
    low = same & (t[None, :] <= t[:, None])
    upp = same & (t[None, :] > t[:, None])
    sel = np.repeat(np.arange(SUB), LANES)[:, None] == (t[None, :] % SUB)
    return jnp.asarray(low, BF16), jnp.asarray(upp, BF16), jnp.asarray(sel, BF16)


def _hgrn(z, offs, lb, gnorm, row0, rows, heads, tb, s_in, out_rows):
    q_off, f_off, v_off, og_off = [o // LANES for o in offs]
    rb0 = row0 // tb
    nt = rows // tb
    nsb = tb // SUB
    low, upp, sel = _hgrn_consts(tb)
    chain = s_in is None

    def col(off):
        return pl.BlockSpec((tb, LANES), lambda h, t: (rb0 + t, off + h))

    const2 = lambda h, t: (0, 0)
    in_specs = [col(q_off), col(f_off), col(v_off), col(og_off),
                pl.BlockSpec((1, LANES), lambda h, t: (0, h)),
                pl.BlockSpec((1, LANES), const2),
                pl.BlockSpec((tb, tb), const2),
                pl.BlockSpec((tb, tb), const2),
                pl.BlockSpec((SUB * LANES, tb), const2)]
    args = [z, z, z, z, lb, gnorm, low, upp, sel]
    scratch = []
    if chain:
        s_shape = (1, heads, LANES, LANES)
        s_spec = pl.BlockSpec((1, 1, LANES, LANES), lambda h, t: (0, h, 0, 0))
        scratch = [pltpu.VMEM((LANES, LANES), F32)]
    else:
        s_shape = s_in.shape
        s_spec = pl.BlockSpec((nsb, 1, LANES, LANES), lambda h, t: (t, h, 0, 0))
        in_specs.append(s_spec)
        args.append(s_in)
    return pl.pallas_call(
        functools.partial(_hgrn_kernel, nsb=nsb, chain=chain),
        grid=(heads, nt),
        in_specs=in_specs,
        out_specs=[pl.BlockSpec((tb, LANES), lambda h, t: (t, h)), s_spec],
        out_shape=[jax.ShapeDtypeStruct((out_rows, heads * LANES), BF16),
                   jax.ShapeDtypeStruct(s_shape, F32)],
        scratch_shapes=scratch,
        compiler_params=_params("parallel", "arbitrary"),
        name="hgrn_chain" if chain else "hgrn_step",
    )(*args)


def _fox_prompt_kernel(qi_ref, ki_ref, q_ref, k_ref, v_ref, cq_ref, ck_ref, o_ref,
                       m_ref, l_ref, acc_ref, *, scale, tq):
    p = pl.program_id(1)
    qi = qi_ref[p]
    ki = ki_ref[p]

    @pl.when(ki == 0)
    def _():
        m_ref[...] = jnp.full_like(m_ref, -jnp.inf)
        l_ref[...] = jnp.zeros_like(l_ref)
        acc_ref[...] = jnp.zeros_like(acc_ref)

    q = (q_ref[...] * scale).astype(BF16)
    s = lax.dot_general(q, k_ref[...].astype(BF16), _NT, preferred_element_type=F32)
    s = s + (cq_ref[0] - ck_ref[0])
    row = lax.broadcasted_iota(jnp.int32, s.shape, 0)
    colk = lax.broadcasted_iota(jnp.int32, s.shape, 1)
    s = jnp.where(colk <= row + (qi - ki) * tq, s, -jnp.inf)
    m_old = m_ref[...]
    m_new = jnp.maximum(m_old, jnp.max(s, axis=-1, keepdims=True))
    alpha = jnp.exp(m_old - m_new)
    pr = jnp.exp(s - m_new)
    l_ref[...] = alpha * l_ref[...] + jnp.sum(pr, axis=-1, keepdims=True)
    acc_ref[...] = alpha * acc_ref[...] + jnp.dot(pr.astype(BF16), v_ref[...].astype(BF16),
                                                  preferred_element_type=F32)
    m_ref[...] = m_new

    @pl.when(ki == qi)
    def _():
        o_ref[...] = (acc_ref[...] / l_ref[...]).astype(o_ref.dtype)


def _fox_prompt(z, offs, ccol, crow, rows, heads, tq, scale):
    q_off, k_off, v_off = [o // LANES for o in offs]
    nq = rows // tq
    pairs = [(i, j) for i in range(nq) for j in range(i + 1)]
    qi = jnp.asarray([p[0] for p in pairs], jnp.int32)
    ki = jnp.asarray([p[1] for p in pairs], jnp.int32)
    grid_spec = pltpu.PrefetchScalarGridSpec(
        num_scalar_prefetch=2,
        grid=(heads, len(pairs)),
        in_specs=[pl.BlockSpec((tq, LANES), lambda h, p, qi, ki: (qi[p], q_off + h)),
                  pl.BlockSpec((tq, LANES), lambda h, p, qi, ki: (ki[p], k_off + h)),
                  pl.BlockSpec((tq, LANES), lambda h, p, qi, ki: (ki[p], v_off + h)),
                  pl.BlockSpec((1, tq, 1), lambda h, p, qi, ki: (h, qi[p], 0)),
                  pl.BlockSpec((1, 1, tq), lambda h, p, qi, ki: (h, 0, ki[p]))],
        out_specs=pl.BlockSpec((tq, LANES), lambda h, p, qi, ki: (qi[p], h)),
        scratch_shapes=[pltpu.VMEM((tq, 1), F32), pltpu.VMEM((tq, 1), F32),
                        pltpu.VMEM((tq, LANES), F32)],
    )
    return pl.pallas_call(
        functools.partial(_fox_prompt_kernel, scale=scale, tq=tq),
        grid_spec=grid_spec,
        out_shape=jax.ShapeDtypeStruct((rows, heads * LANES), BF16),
        compiler_params=_params("parallel", "arbitrary"),
        name="fox_prompt",
    )(qi, ki, z, z, z, ccol, crow)


def _fox_sample_kernel(q_ref, kc_ref, vc_ref, kn_ref, vn_ref, cq_ref, ckc_ref, ckn_ref, o_ref,
                       *, scale, t_new):
    q = (q_ref[...] * scale).astype(BF16)
    cq = cq_ref[0]
    s_c = lax.dot_general(q, kc_ref[0].astype(BF16), _NT, preferred_element_type=F32)
    s_c = s_c + (cq - ckc_ref[0])
    s_n = lax.dot_general(q, kn_ref[0].astype(BF16), _NT, preferred_element_type=F32)
    s_n = s_n + (cq - ckn_ref[0])
    row = lax.broadcasted_iota(jnp.int32, s_n.shape, 0)
    colk = lax.broadcasted_iota(jnp.int32, s_n.shape, 1)
    s_n = jnp.where((colk <= row) & (colk < t_new), s_n, -jnp.inf)
    m = jnp.maximum(jnp.max(s_c, axis=-1, keepdims=True), jnp.max(s_n, axis=-1, keepdims=True))
    p_c = jnp.exp(s_c - m)
    p_n = jnp.exp(s_n - m)
    l = jnp.sum(p_c, axis=-1, keepdims=True) + jnp.sum(p_n, axis=-1, keepdims=True)
    o = jnp.dot(p_c.astype(BF16), vc_ref[0].astype(BF16), preferred_element_type=F32)
    o += jnp.dot(p_n.astype(BF16), vn_ref[0].astype(BF16), preferred_element_type=F32)
    o_ref[...] = (o / l).astype(o_ref.dtype)


def _fox_sample(z, q_off, row0, k_cache, v_cache, k_new, v_new, ccol, crow_c, crow_n,
                heads, t_new, scale):
    nb, past, _ = k_cache.shape
    npad = k_new.shape[1]
    qb = q_off // LANES
    rb0 = row0 // t_new
    return pl.pallas_call(
        functools.partial(_fox_sample_kernel, scale=scale, t_new=t_new),
        grid=(nb, heads),
        in_specs=[pl.BlockSpec((t_new, LANES), lambda b, h: (rb0 + b, qb + h)),
                  pl.BlockSpec((1, past, LANES), lambda b, h: (b, 0, h)),
                  pl.BlockSpec((1, past, LANES), lambda b, h: (b, 0, h)),
                  pl.BlockSpec((1, npad, LANES), lambda b, h: (b, 0, h)),
                  pl.BlockSpec((1, npad, LANES), lambda b, h: (b, 0, h)),
                  pl.BlockSpec((1, t_new, 1), lambda b, h: (b * heads + h, 0, 0)),
                  pl.BlockSpec((1, 1, past), lambda b, h: (b * heads + h, 0, 0)),
                  pl.BlockSpec((1, 1, npad), lambda b, h: (b * heads + h, 0, 0))],
        out_specs=pl.BlockSpec((t_new, LANES), lambda b, h: (b, h)),
        out_shape=jax.ShapeDtypeStruct((nb * t_new, heads * LANES), BF16),
        compiler_params=_params("parallel", "parallel"),
        name="fox_sample",
    )(z, k_cache, v_cache, k_new, v_new, ccol, crow_c, crow_n)


def kernel(x_prompt, x_sample, cache_fox_k, cache_fox_v, cache_fox_logf, state_hgrn, p_prompt, p_sample, ffn1_pre, ffn1_post, ffn1_wg, ffn1_wu, ffn1_wd, mix_pre, mix_post, w_in, fox_fbias, hgrn_lb_logits, hgrn_gnorm, w_branch_a, w_branch_b, w_out, ffn2_pre, ffn2_post, ffn2_wg, ffn2_wu, ffn2_wd, ple_pre, ple_post, w_ple_gate, w_ple_proj):
    depth = w_in.shape[0]
    _, seq, d = x_prompt.shape
    nb, t_new, _ = x_sample.shape
    past = cache_fox_k.shape[2]
    heads, dh = cache_fox_k.shape[3], cache_fox_k.shape[4]
    h_a, dk, dv = state_hgrn.shape[2], state_hgrn.shape[3], state_hgrn.shape[4]
    assert dk == LANES and dv == LANES and dh == LANES and t_new == SUB
    w_ak, w_av, w_b = h_a * dk, h_a * dv, heads * dh
    o_qa, o_fa, o_ia, o_og = 0, w_ak, 2 * w_ak, 2 * w_ak + w_av
    o_qb = 2 * w_ak + 2 * w_av
    o_kb, o_vb = o_qb + w_b, o_qb + 2 * w_b
    o_fl = o_qb + 3 * w_b
    o_ga = o_fl
    o_gb = o_ga + d
    scale = dh ** -0.5
    m_rows = seq + nb * t_new
    tm = 768
    hgrn_tb = 256
    fox_tq = 512
    cs_tb = 512
    n_new = nb * t_new

    h = jnp.concatenate([x_prompt.reshape(seq, d), x_sample.reshape(n_new, d)], axis=0)
    row = lambda v: v.reshape(1, -1)
    lb_all = jnp.cumsum(jax.nn.softmax(hgrn_lb_logits.astype(F32), axis=0), axis=0)

    new_k, new_v, new_logf, st_p, st_s = [], [], [], [], []
    for l in range(depth):
        a = _ffn_up(h, row(ffn1_pre[l]), ffn1_wg[l].astype(BF16), ffn1_wu[l].astype(BF16), tm, 512)
        h = _mm_post(a, ffn1_wd[l].astype(BF16), h, row(ffn1_post[l]), 0.5, tm, 512)

        w_l = w_in[l]
        w_main = jnp.concatenate([w_l[:, :o_fl], w_l[:, o_fl + heads:]], axis=1).astype(BF16)
        z = _pn_mm(h, row(mix_pre[l]), w_main, tm, 1024)
        w_fl = jnp.pad(w_l[:, o_fl:o_fl + heads], ((0, 0), (0, LANES - heads))).astype(BF16)
        b_fl = jnp.pad(fox_fbias[l], (0, LANES - heads)).reshape(1, LANES)
        logf = _fox_logf(h, row(mix_pre[l]), w_fl, b_fl, tm)

        c_p = _cumsum_rows(logf[:seq].reshape(1, seq, LANES), cs_tb)[0, :, :heads]
        ccol_p = c_p.T.reshape(heads, seq, 1)
        crow_p = c_p.T.reshape(heads, 1, seq)
        logf_new = logf[seq:, :heads].reshape(nb, t_new, heads)
        t_all = past + t_new
        t_pad = -(-t_all // cs_tb) * cs_tb
        lf_all = jnp.concatenate([cache_fox_logf[l].astype(F32), logf_new], axis=1)
        lf_all = jnp.pad(lf_all, ((0, 0), (0, t_pad - t_all), (0, LANES - heads)))
        c_s = _cumsum_rows(lf_all, cs_tb)[:, :t_all, :heads]
        c_s = jnp.swapaxes(c_s, 1, 2).reshape(nb * heads, t_all)
        crow_c = c_s[:, :past].reshape(nb * heads, 1, past)
        c_n = c_s[:, past:]
        ccol_n = c_n.reshape(nb * heads, t_new, 1)
        crow_n = jnp.pad(c_n, ((0, 0), (0, LANES - t_new))).reshape(nb * heads, 1, LANES)

        lb = row(lb_all[l])
        gn = row(hgrn_gnorm[l])
        offs_a = (o_qa, o_fa, o_ia, o_og)
        oa_p, s_p = _hgrn(z, offs_a, lb, gn, 0, seq, h_a, hgrn_tb, None, seq)
        oa_s, s_s = _hgrn(z, offs_a, lb, gn, seq, n_new, h_a, n_new, state_hgrn[l].astype(F32), n_new)

        ob_p = _fox_prompt(z, (o_qb, o_kb, o_vb), ccol_p, crow_p, seq, heads, fox_tq, scale)
        kv_new = z[seq:, o_kb:o_vb + w_b].reshape(nb, t_new, 2 * w_b)
        kv_new = jnp.pad(kv_new, ((0, 0), (0, LANES - t_new), (0, 0)))
        ob_s = _fox_sample(z, o_qb, seq, cache_fox_k[l].reshape(nb, past, w_b),
                           cache_fox_v[l].reshape(nb, past, w_b), kv_new[..., :w_b], kv_new[..., w_b:],
                           ccol_n, crow_c, crow_n, heads, t_new, scale)

        oa = jnp.concatenate([oa_p, oa_s], axis=0)
        ob = jnp.concatenate([ob_p, ob_s], axis=0)
        mg = _mix(oa, ob, z, o_ga, o_gb, w_branch_a[l].astype(BF16), w_branch_b[l].astype(BF16), tm, 512)
        h = _mm_post(mg, w_out[l].astype(BF16), h, row(mix_post[l]), 1.0, tm, 512)

        a = _ffn_up(h, row(ffn2_pre[l]), ffn2_wg[l].astype(BF16), ffn2_wu[l].astype(BF16), tm, 512)
        h = _mm_post(a, ffn2_wd[l].astype(BF16), h, row(ffn2_post[l]), 0.5, tm, 512)

        pe = jnp.concatenate([p_prompt[l].reshape(seq, -1), p_sample[l].reshape(n_new, -1)], axis=0)
        h = _ple(h, row(ple_pre[l]), w_ple_gate[l].astype(BF16), pe, w_ple_proj[l].astype(BF16),
                 row(ple_post[l]), 256)

        new_k.append(z[:, o_kb:o_kb + w_b])
        new_v.append(z[:, o_vb:o_vb + w_b])
        new_logf.append(logf[:, :heads])
        st_p.append(s_p)
        st_s.append(s_s)

    dt_p, dt_s = x_prompt.dtype, x_sample.dtype
    stack = lambda xs: jnp.stack(xs)
    k_all, v_all, lf = stack(new_k), stack(new_v), stack(new_logf)
    return (h[:seq].reshape(x_prompt.shape), h[seq:].reshape(x_sample.shape),
            k_all[:, :seq].reshape(depth, 1, seq, heads, dh).astype(dt_p),
            v_all[:, :seq].reshape(depth, 1, seq, heads, dh).astype(dt_p),
            lf[:, :seq].reshape(depth, 1, seq, heads).astype(dt_p),
            jnp.concatenate(st_p, axis=0).reshape(depth, 1, h_a, dk, dv).astype(dt_p),
            k_all[:, seq:].reshape(depth, nb, t_new, heads, dh).astype(dt_s),
            v_all[:, seq:].reshape(depth, nb, t_new, heads, dh).astype(dt_s),
            lf[:, seq:].reshape(depth, nb, t_new, heads).astype(dt_s),
            jnp.stack(st_s).astype(dt_s))
```

```python
import functools

import numpy as np
import jax
import jax.numpy as jnp
from jax import lax
from jax.experimental import pallas as pl
from jax.experimental.pallas import tpu as pltpu

F32 = jnp.float32
BF16 = jnp.bfloat16
EPS = 1e-6
LANES = 128
SUB = 16
VMEM_LIMIT = 56 * 1024 * 1024


def _params(*sem):
    return pltpu.CompilerParams(dimension_semantics=sem, vmem_limit_bytes=VMEM_LIMIT)


def _sigmoid(x):
    return 1.0 / (1.0 + jnp.exp(-x))


def _silu(x):
    return x * _sigmoid(x)


def _log_sigmoid(x):
    return jnp.minimum(x, 0.0) - jnp.log(1.0 + jnp.exp(-jnp.abs(x)))


def _rms(x):
    return x * lax.rsqrt(jnp.mean(x * x, axis=-1, keepdims=True) + EPS)


def _split3(x):
    x1 = x.astype(BF16)
    r1 = x - x1.astype(F32)
    x2 = r1.astype(BF16)
    x3 = (r1 - x2.astype(F32)).astype(BF16)
    return x1, x2, x3


def _dot3(m, parts):
    acc = jnp.dot(m, parts[0], preferred_element_type=F32)
    acc += jnp.dot(m, parts[1], preferred_element_type=F32)
    acc += jnp.dot(m, parts[2], preferred_element_type=F32)
    return acc


_NT = (((1,), (1,)), ((), ()))
_TN = (((0,), (0,)), ((), ()))


def _prenorm_to_scratch(x_ref, nw_ref, xn_ref):
    @pl.when(pl.program_id(1) == 0)
    def _():
        xn_ref[...] = (_rms(x_ref[...]) * nw_ref[...]).astype(BF16)


def _ffn_up_kernel(x_ref, nw_ref, wg_ref, wu_ref, o_ref, xn_ref):
    _prenorm_to_scratch(x_ref, nw_ref, xn_ref)
    xn = xn_ref[...]
    g = jnp.dot(xn, wg_ref[...], preferred_element_type=F32)
    u = jnp.dot(xn, wu_ref[...], preferred_element_type=F32)
    o_ref[...] = (_silu(g) * u).astype(o_ref.dtype)


def _ffn_up(h, nw, wg, wu, tm, tf):
    m, d = h.shape
    f = wg.shape[1]
    return pl.pallas_call(
        _ffn_up_kernel,
        grid=(m // tm, f // tf),
        in_specs=[pl.BlockSpec((tm, d), lambda i, j: (i, 0)),
                  pl.BlockSpec((1, d), lambda i, j: (0, 0)),
                  pl.BlockSpec((d, tf), lambda i, j: (0, j)),
                  pl.BlockSpec((d, tf), lambda i, j: (0, j))],
        out_specs=pl.BlockSpec((tm, tf), lambda i, j: (i, j)),
        out_shape=jax.ShapeDtypeStruct((m, f), BF16),
        scratch_shapes=[pltpu.VMEM((tm, d), BF16)],
        compiler_params=_params("parallel", "arbitrary"),
        name="ffn_up",
    )(h, nw, wg, wu)


def _mm_post_kernel(a_ref, w_ref, h_ref, pw_ref, o_ref, acc_ref, *, scale, nk):
    k = pl.program_id(1)

    @pl.when(k == 0)
    def _():
        acc_ref[...] = jnp.zeros_like(acc_ref)

    acc_ref[...] += jnp.dot(a_ref[...], w_ref[...], preferred_element_type=F32)

    @pl.when(k == nk - 1)
    def _():
        o_ref[...] = h_ref[...] + scale * (_rms(acc_ref[...]) * pw_ref[...])


def _mm_post(a, w, h, pw, scale, tm, tk):
    m, kdim = a.shape
    d = w.shape[1]
    nk = kdim // tk
    return pl.pallas_call(
        functools.partial(_mm_post_kernel, scale=scale, nk=nk),
        grid=(m // tm, nk),
        in_specs=[pl.BlockSpec((tm, tk), lambda i, k: (i, k)),
                  pl.BlockSpec((tk, d), lambda i, k: (k, 0)),
                  pl.BlockSpec((tm, d), lambda i, k: (i, 0)),
                  pl.BlockSpec((1, d), lambda i, k: (0, 0))],
        out_specs=pl.BlockSpec((tm, d), lambda i, k: (i, 0)),
        out_shape=jax.ShapeDtypeStruct((m, d), F32),
        scratch_shapes=[pltpu.VMEM((tm, d), F32)],
        compiler_params=_params("parallel", "arbitrary"),
        name="mm_post",
    )(a, w, h, pw)


def _pn_mm_kernel(x_ref, nw_ref, w_ref, o_ref, xn_ref):
    _prenorm_to_scratch(x_ref, nw_ref, xn_ref)
    o_ref[...] = jnp.dot(xn_ref[...], w_ref[...], preferred_element_type=F32)


def _pn_mm(h, nw, w, tm, tn):
    m, d = h.shape
    n = w.shape[1]
    return pl.pallas_call(
        _pn_mm_kernel,
        grid=(m // tm, n // tn),
        in_specs=[pl.BlockSpec((tm, d), lambda i, j: (i, 0)),
                  pl.BlockSpec((1, d), lambda i, j: (0, 0)),
                  pl.BlockSpec((d, tn), lambda i, j: (0, j))],
        out_specs=pl.BlockSpec((tm, tn), lambda i, j: (i, j)),
        out_shape=jax.ShapeDtypeStruct((m, n), F32),
        scratch_shapes=[pltpu.VMEM((tm, d), BF16)],
        compiler_params=_params("parallel", "arbitrary"),
        name="w_in_proj",
    )(h, nw, w)


def _fox_logf_kernel(x_ref, nw_ref, w_ref, b_ref, o_ref):
    xn = (_rms(x_ref[...]) * nw_ref[...]).astype(BF16)
    z = jnp.dot(xn, w_ref[...], preferred_element_type=F32)
    o_ref[...] = _log_sigmoid(z + b_ref[...])


def _fox_logf(h, nw, w, bias, tm):
    m, d = h.shape
    n = w.shape[1]
    return pl.pallas_call(
        _fox_logf_kernel,
        grid=(m // tm,),
        in_specs=[pl.BlockSpec((tm, d), lambda i: (i, 0)),
                  pl.BlockSpec((1, d), lambda i: (0, 0)),
                  pl.BlockSpec((d, n), lambda i: (0, 0)),
                  pl.BlockSpec((1, n), lambda i: (0, 0))],
        out_specs=pl.BlockSpec((tm, n), lambda i: (i, 0)),
        out_shape=jax.ShapeDtypeStruct((m, n), F32),
        compiler_params=_params("parallel"),
        name="fox_logf",
    )(h, nw, w, bias)


def _cumsum_kernel(x_ref, tri_ref, o_ref, carry_ref):
    @pl.when(pl.program_id(1) == 0)
    def _():
        carry_ref[...] = jnp.zeros_like(carry_ref)

    c = _dot3(tri_ref[...], _split3(x_ref[0])) + carry_ref[...]
    o_ref[0] = c
    carry_ref[...] = c[-1:, :]


def _cumsum_rows(x, tb):
    b, t, n = x.shape
    r = np.arange(tb)
    tri = jnp.asarray(r[None, :] <= r[:, None], BF16)
    return pl.pallas_call(
        _cumsum_kernel,
        grid=(b, t // tb),
        in_specs=[pl.BlockSpec((1, tb, n), lambda i, j: (i, j, 0)),
                  pl.BlockSpec((tb, tb), lambda i, j: (0, 0))],
        out_specs=pl.BlockSpec((1, tb, n), lambda i, j: (i, j, 0)),
        out_shape=jax.ShapeDtypeStruct((b, t, n), F32),
        scratch_shapes=[pltpu.VMEM((1, n), F32)],
        compiler_params=_params("parallel", "arbitrary"),
        name="cumsum_rows",
    )(x, tri)


def _mix_kernel(oa_ref, ob_ref, ga_ref, gb_ref, wa_ref, wb_ref, o_ref):
    ya = jnp.dot(oa_ref[...], wa_ref[...], preferred_element_type=F32)
    yb = jnp.dot(ob_ref[...], wb_ref[...], preferred_element_type=F32)
    o_ref[...] = (_sigmoid(ga_ref[...]) * ya + _sigmoid(gb_ref[...]) * yb).astype(o_ref.dtype)


def _mix(oa, ob, z, ga_off, gb_off, wa, wb, tm, tn):
    m, d = oa.shape
    n = wa.shape[1]
    ga_blk, gb_blk = ga_off // tn, gb_off // tn
    return pl.pallas_call(
        _mix_kernel,
        grid=(m // tm, n // tn),
        in_specs=[pl.BlockSpec((tm, d), lambda i, j: (i, 0)),
                  pl.BlockSpec((tm, d), lambda i, j: (i, 0)),
                  pl.BlockSpec((tm, tn), lambda i, j: (i, ga_blk + j)),
                  pl.BlockSpec((tm, tn), lambda i, j: (i, gb_blk + j)),
                  pl.BlockSpec((d, tn), lambda i, j: (0, j)),
                  pl.BlockSpec((d, tn), lambda i, j: (0, j))],
        out_specs=pl.BlockSpec((tm, tn), lambda i, j: (i, j)),
        out_shape=jax.ShapeDtypeStruct((m, n), BF16),
        compiler_params=_params("parallel", "arbitrary"),
        name="mix_gate",
    )(oa, ob, z, z, wa, wb)


def _ple_kernel(h_ref, nw_ref, wg_ref, pe_ref, wp_ref, pw_ref, o_ref):
    x = h_ref[...]
    xn = (_rms(x) * nw_ref[...]).astype(BF16)
    gate = _sigmoid(jnp.dot(xn, wg_ref[...], preferred_element_type=F32))
    e = jnp.dot(pe_ref[...].astype(BF16), wp_ref[...], preferred_element_type=F32)
    o_ref[...] = x + _rms(gate * e) * pw_ref[...]


def _ple(h, nw, wg, pe, wp, pw, tm):
    m, d = h.shape
    p = pe.shape[1]
    return pl.pallas_call(
        _ple_kernel,
        grid=(m // tm,),
        in_specs=[pl.BlockSpec((tm, d), lambda i: (i, 0)),
                  pl.BlockSpec((1, d), lambda i: (0, 0)),
                  pl.BlockSpec((d, d), lambda i: (0, 0)),
                  pl.BlockSpec((tm, p), lambda i: (i, 0)),
                  pl.BlockSpec((p, d), lambda i: (0, 0)),
                  pl.BlockSpec((1, d), lambda i: (0, 0))],
        out_specs=pl.BlockSpec((tm, d), lambda i: (i, 0)),
        out_shape=jax.ShapeDtypeStruct((m, d), F32),
        compiler_params=_params("parallel"),
        name="ple",
    )(h, nw, wg, pe, wp, pw)


def _hgrn_kernel(*refs, nsb, ch, chain):
    if chain:
        (q_ref, f_ref, v_ref, og_ref, lb_ref, gn_ref, band_ref, sel_ref,
         o_ref, s_out_ref, st_ref) = refs
    else:
        (q_ref, f_ref, v_ref, og_ref, lb_ref, gn_ref, band_ref, sel_ref, s_in_ref,
         o_ref, s_out_ref) = refs
    tb = nsb * SUB
    cr = ch * SUB
    dk = q_ref.shape[1]

    q = _silu(q_ref[...])
    lb = lb_ref[...]
    f = lb + (1.0 - lb) * _sigmoid(f_ref[...])
    k = 1.0 - f
    t_idx = lax.broadcasted_iota(jnp.int32, (tb, 1), 0)
    t_off = t_idx % SUB
    sub_pos = (t_idx // SUB) % ch

    w = jnp.log(f)
    sh = 1
    while sh < SUB:
        w = w + jnp.where(t_off >= sh, pltpu.roll(w, sh, 0), 0.0)
        sh *= 2
    w3 = w.reshape(nsb, SUB, dk)
    totb = jnp.broadcast_to(w3[:, SUB - 1:SUB, :], (nsb, SUB, dk)).reshape(tb, dk)
    u = totb - w

    e_start = jnp.zeros_like(w)
    f_end = jnp.zeros_like(w)
    between = jnp.zeros_like(w)
    lhs = [(q * jnp.exp(w)).astype(BF16)]
    for dlt in range(1, ch):
        prev = pltpu.roll(totb, dlt * SUB, 0)
        nxt = pltpu.roll(totb, tb - dlt * SUB, 0)
        e_start = e_start + jnp.where(sub_pos >= dlt, prev, 0.0)
        f_end = f_end + jnp.where(sub_pos < ch - dlt, nxt, 0.0)
        between = between + prev
        if dlt + 1 < ch:
            lhs.append((q * jnp.exp(w + between)).astype(BF16))
    kh = (k * jnp.exp(u)).astype(BF16)
    vb = v_ref[...].astype(BF16)
    if ch > 1:
        qs = (q * jnp.exp(w + e_start)).astype(BF16)
        ke = (k * jnp.exp(u + f_end)).astype(BF16)
    else:
        qs, ke = lhs[0], kh
    ctot = e_start + totb + f_end

    nchunk = nsb // ch
    upd = [lax.dot_general(vb[c * cr:(c + 1) * cr], ke[c * cr:(c + 1) * cr], _TN,
                           preferred_element_type=F32) for c in range(nchunk)]

    half = SUB // 2
    q3 = q.reshape(nsb, SUB, dk)
    wk3 = (w - jnp.log(k)).reshape(nsb, SUB, dk)
    pieces = []
    for s in range(SUB):
        lo = 0 if s < half else half
        p = q3[:, lo:, :] * jnp.exp(jnp.minimum(w3[:, lo:, :] - wk3[:, s:s + 1, :], 0.0))
        if lo:
            p = jnp.concatenate([jnp.zeros((nsb, lo, dk), F32), p], axis=1)
        pieces.append(p.reshape(tb, dk).astype(BF16))
    band = band_ref[...]
    a = jnp.dot(jnp.concatenate(pieces, axis=1), sel_ref[...], preferred_element_type=F32)
    a = jnp.where(band == 1.0, a, 0.0)
    if ch > 1:
        r = lax.dot_general(jnp.concatenate(lhs, axis=0), kh, _NT, preferred_element_type=F32)
        for dlt in range(1, ch):
            a = jnp.where(band == dlt + 1.0, r[(dlt - 1) * tb:dlt * tb], a)
    o_in = jnp.dot(a.astype(BF16), vb, preferred_element_type=F32)

    if chain:
        @pl.when(pl.program_id(1) == 0)
        def _():
            st_ref[...] = jnp.zeros_like(st_ref)
        st = st_ref[...]
    outs = []
    for c in range(nchunk):
        rows = slice(c * cr, (c + 1) * cr)
        if not chain:
            st = s_in_ref[c, 0].T
        outs.append(o_in[rows] + lax.dot_general(qs[rows], st.astype(BF16), _NT,
                                                 preferred_element_type=F32))
        st = st * jnp.exp(ctot[c * cr:c * cr + 1, :]) + upd[c]
        if not chain:
            s_out_ref[c, 0] = st.T
    if chain:
        st_ref[...] = st

        @pl.when(pl.program_id(1) == pl.num_programs(1) - 1)
        def _():
            s_out_ref[0, 0] = st.T

    o = jnp.concatenate(outs, axis=0)
    o_ref[...] = (_rms(o) * gn_ref[...] * _silu(og_ref[...])).astype(o_ref.dtype)


def _hgrn_consts(tb, ch):
    t = np.arange(tb)
    blk = t // SUB
    d = blk[:, None] - blk[None, :]
    same_chunk = (blk[:, None] // ch) == (blk[None, :] // ch)
    band = np.where((d == 0) & (t[None, :] <= t[:, None]), 1, 0)
    band = np.where(same_chunk & (d >= 1), d + 1, band)
    sel = np.repeat(np.arange(SUB), LANES)[:, None] == (t[None, :] % SUB)
    return jnp.asarray(band, F32), jnp.asarray(sel, BF16)


def _hgrn(z, offs, lb, gnorm, row0, rows, heads, tb, ch, s_in, out_rows):
    q_off, f_off, v_off, og_off = [o // LANES for o in offs]
    rb0 = row0 // tb
    nt = rows // tb
    nsb = tb // SUB
    band, sel = _hgrn_consts(tb, ch)
    chain = s_in is None

    def col(off):
        return pl.BlockSpec((tb, LANES), lambda h, t: (rb0 + t, off + h))

    const2 = lambda h, t: (0, 0)
    in_specs = [col(q_off), col(f_off), col(v_off), col(og_off),
                pl.BlockSpec((1, LANES), lambda h, t: (0, h)),
                pl.BlockSpec((1, LANES), const2),
                pl.BlockSpec((tb, tb), const2),
                pl.BlockSpec((SUB * LANES, tb), const2)]
    args = [z, z, z, z, lb, gnorm, band, sel]
    scratch = []
    if chain:
        s_shape = (1, heads, LANES, LANES)
        s_spec = pl.BlockSpec((1, 1, LANES, LANES), lambda h, t: (0, h, 0, 0))
        scratch = [pltpu.VMEM((LANES, LANES), F32)]
    else:
        s_shape = s_in.shape
        s_spec = pl.BlockSpec((nsb // ch, 1, LANES, LANES), lambda h, t: (t, h, 0, 0))
        in_specs.append(s_spec)
        args.append(s_in)
    return pl.pallas_call(
        functools.partial(_hgrn_kernel, nsb=nsb, ch=ch, chain=chain),
        grid=(heads, nt),
        in_specs=in_specs,
        out_specs=[pl.BlockSpec((tb, LANES), lambda h, t: (t, h)), s_spec],
        out_shape=[jax.ShapeDtypeStruct((out_rows, heads * LANES), BF16),
                   jax.ShapeDtypeStruct(s_shape, F32)],
        scratch_shapes=scratch,
        compiler_params=_params("parallel", "arbitrary"),
        name="hgrn_chain" if chain else "hgrn_step",
    )(*args)


def _fold_rows(x, op):
    while x.shape[0] % 16 == 0 and x.shape[0] > 64:
        half = x.shape[0] // 2
        x = op(x[:half], x[half:])
    red = jnp.max if op is jnp.maximum else jnp.sum
    return red(x, axis=0, keepdims=True)


def _softmax_step_t(st, cols, vt, m_ref, l_ref, acc_ref):
    m_old = m_ref[:, cols]
    m_new = jnp.maximum(m_old, _fold_rows(st, jnp.maximum))
    alpha = jnp.exp2(m_old - m_new)
    p = jnp.exp2(st - m_new)
    l_ref[:, cols] = alpha * l_ref[:, cols] + _fold_rows(p, jnp.add)
    acc_ref[:, cols] = alpha * acc_ref[:, cols] + jnp.dot(vt, p.astype(BF16),
                                                          preferred_element_type=F32)
    m_ref[:, cols] = m_new


def _fox_prompt_kernel(q_ref, cq_ref, k_ref, ck_ref, v_ref, o_ref,
                       qa_ref, ka_ref, vt_ref, m_ref, l_ref, acc_ref, s_ref, *, scale2, tq, nsub):
    qi = pl.program_id(1)
    nkb = ka_ref.shape[0]
    sr = tq // nsub

    @pl.when(qi == 0)
    def _():
        for c in range(nkb):
            rows = slice(c * tq, (c + 1) * tq)
            ka_ref[c, :, :LANES] = k_ref[rows, :].astype(BF16)
            ka_ref[c, :, LANES:] = ck_ref[0, rows, :]
            vt_ref[c] = v_ref[rows, :].T.astype(BF16)

    qa_ref[:LANES, :] = (q_ref[...] * scale2).T.astype(BF16)
    qa_ref[LANES:, :] = cq_ref[0]
    m_ref[...] = jnp.full_like(m_ref, -jnp.inf)
    l_ref[...] = jnp.zeros_like(l_ref)
    acc_ref[...] = jnp.zeros_like(acc_ref)

    def scores(j, slot):
        s_ref[slot] = jnp.dot(ka_ref[j], qa_ref[...], preferred_element_type=F32)

    def softmax_pv(j, slot, diagonal):
        for r in range(nsub):
            cols = slice(r * sr, (r + 1) * sr)
            n = (r + 1) * sr if diagonal else tq
            st = s_ref[slot, :n, cols]
            if diagonal:
                key = lax.broadcasted_iota(jnp.int32, st.shape, 0)
                qry = lax.broadcasted_iota(jnp.int32, st.shape, 1) + r * sr
                st = jnp.where(key <= qry, st, -jnp.inf)
            _softmax_step_t(st, cols, vt_ref[j, :, :n], m_ref, l_ref, acc_ref)

    scores(0, 0)

    def two_blocks(jj, carry):
        j = 2 * jj
        scores(j + 1, 1)
        softmax_pv(j, 0, False)
        scores(j + 2, 0)
        softmax_pv(j + 1, 1, False)
        return carry

    lax.fori_loop(0, qi // 2, two_blocks, 0)

    @pl.when(qi % 2 == 0)
    def _():
        softmax_pv(qi, 0, True)

    @pl.when(qi % 2 == 1)
    def _():
        scores(qi, 1)
        softmax_pv(qi - 1, 0, False)
        softmax_pv(qi, 1, True)

    o_ref[...] = (acc_ref[...] / l_ref[...]).T.astype(o_ref.dtype)


def _fox_prompt(z, offs, caug_qt, caug_k, rows, heads, tq, nsub, scale2):
    q_off, k_off, v_off = [o // LANES for o in offs]
    nkb = rows // tq
    return pl.pallas_call(
        functools.partial(_fox_prompt_kernel, scale2=scale2, tq=tq, nsub=nsub),
        grid=(heads, nkb),
        in_specs=[pl.BlockSpec((tq, LANES), lambda h, i: (i, q_off + h)),
                  pl.BlockSpec((1, LANES, tq), lambda h, i: (h, 0, i)),
                  pl.BlockSpec((rows, LANES), lambda h, i: (0, k_off + h)),
                  pl.BlockSpec((1, rows, LANES), lambda h, i: (h, 0, 0)),
                  pl.BlockSpec((rows, LANES), lambda h, i: (0, v_off + h))],
        out_specs=pl.BlockSpec((tq, LANES), lambda h, i: (i, h)),
        out_shape=jax.ShapeDtypeStruct((rows, heads * LANES), BF16),
        scratch_shapes=[pltpu.VMEM((2 * LANES, tq), BF16),
                        pltpu.VMEM((nkb, tq, 2 * LANES), BF16),
                        pltpu.VMEM((nkb, LANES, tq), BF16),
                        pltpu.VMEM((1, tq), F32), pltpu.VMEM((1, tq), F32),
                        pltpu.VMEM((LANES, tq), F32),
                        pltpu.VMEM((2, tq, tq), F32)],
        compiler_params=_params("parallel", "arbitrary"),
        name="fox_prompt",
    )(z, caug_qt, z, caug_k, z)


HEAD_GROUP = 8


def _fox_sample_kernel(q_ref, kc_ref, vc_ref, kn_ref, vn_ref, cq_ref, ckc_ref, ckn_ref, pen_ref,
                       o_ref, q8_ref, m_ref, l_ref, acc_ref, *, scale, t_new):
    j = pl.program_id(1)
    heads = kc_ref.shape[1]
    ngrp = heads // HEAD_GROUP
    gq = HEAD_GROUP * t_new

    @pl.when(j == 0)
    def _():
        for h in range(heads):
            q8_ref[h // HEAD_GROUP, (h % HEAD_GROUP) * t_new:(h % HEAD_GROUP + 1) * t_new, :] = (
                q_ref[:, h * LANES:(h + 1) * LANES] * scale).astype(BF16)
        m_ref[...] = jnp.full_like(m_ref, -jnp.inf)
        l_ref[...] = jnp.zeros_like(l_ref)
        acc_ref[...] = jnp.zeros_like(acc_ref)

    def update(g, s, vv):
        m_old = m_ref[g]
        m_new = jnp.maximum(m_old, jnp.max(s, axis=-1, keepdims=True))
        alpha = jnp.exp(m_old - m_new)
        p = jnp.exp(s - m_new)
        l_ref[g] = alpha * l_ref[g] + jnp.sum(p, axis=-1, keepdims=True)
        acc_ref[g] = alpha * acc_ref[g] + jnp.dot(p.astype(BF16), vv, preferred_element_type=F32)
        m_ref[g] = m_new

    def group_rows(ref, g):
        x = ref[:, g * HEAD_GROUP:(g + 1) * HEAD_GROUP, :]
        return x.reshape(x.shape[0] * HEAD_GROUP, x.shape[2]).astype(BF16)

    for g in range(ngrp):
        s = lax.dot_general(q8_ref[g], group_rows(kc_ref, g), _NT, preferred_element_type=F32)
        update(g, s + pen_ref[...] + (cq_ref[0, g] - ckc_ref[0, g]), group_rows(vc_ref, g))

    @pl.when(j == pl.num_programs(1) - 1)
    def _():
        for g in range(ngrp):
            s = lax.dot_general(q8_ref[g], group_rows(kn_ref, g), _NT, preferred_element_type=F32)
            s = s + pen_ref[:, :gq] + (cq_ref[0, g] - ckn_ref[0, g])
            qry = lax.broadcasted_iota(jnp.int32, s.shape, 0) % t_new
            key = lax.broadcasted_iota(jnp.int32, s.shape, 1) // HEAD_GROUP
            update(g, jnp.where(key <= qry, s, -jnp.inf), group_rows(vn_ref, g))
            o = acc_ref[g] / l_ref[g]
            for hl in range(HEAD_GROUP):
                h = g * HEAD_GROUP + hl
                o_ref[:, h * LANES:(h + 1) * LANES] = o[hl * t_new:(hl + 1) * t_new].astype(o_ref.dtype)


def _fox_sample(z, q_off, row0, k_cache, v_cache, k_new, v_new, c_all, t_new, tkb, scale):
    nb, past, heads, _ = k_cache.shape
    ngrp = heads // HEAD_GROUP
    gq = HEAD_GROUP * t_new
    w_b = heads * LANES
    rb0 = row0 // t_new
    grp = lambda c: jnp.swapaxes(c.reshape(nb, c.shape[1], ngrp, HEAD_GROUP), 1, 2)
    ck_c = grp(c_all[:, :past]).reshape(nb, ngrp, 1, past * HEAD_GROUP)
    c_n = grp(c_all[:, past:])
    ck_n = c_n.reshape(nb, ngrp, 1, gq)
    cq = jnp.swapaxes(c_n, 2, 3).reshape(nb, ngrp, gq, 1)
    r = np.arange(gq)[:, None] // t_new
    c = np.arange(tkb * HEAD_GROUP)[None, :] % HEAD_GROUP
    pen = jnp.asarray(np.where(r == c, 0.0, -1e30), F32)
    cache_spec = pl.BlockSpec((None, tkb, heads, LANES), lambda b, j: (b, j, 0, 0))
    new_spec = pl.BlockSpec((None, t_new, heads, LANES), lambda b, j: (b, 0, 0, 0))
    return pl.pallas_call(
        functools.partial(_fox_sample_kernel, scale=scale, t_new=t_new),
        grid=(nb, past // tkb),
        in_specs=[pl.BlockSpec((t_new, w_b), lambda b, j: (rb0 + b, q_off // w_b)),
                  cache_spec, cache_spec, new_spec, new_spec,
                  pl.BlockSpec((1, ngrp, gq, 1), lambda b, j: (b, 0, 0, 0)),
                  pl.BlockSpec((1, ngrp, 1, tkb * HEAD_GROUP), lambda b, j: (b, 0, 0, j)),
                  pl.BlockSpec((1, ngrp, 1, gq), lambda b, j: (b, 0, 0, 0)),
                  pl.BlockSpec((gq, tkb * HEAD_GROUP), lambda b, j: (0, 0))],
        out_specs=pl.BlockSpec((t_new, w_b), lambda b, j: (b, 0)),
        out_shape=jax.ShapeDtypeStruct((nb * t_new, w_b), BF16),
        scratch_shapes=[pltpu.VMEM((ngrp, gq, LANES), BF16),
                        pltpu.VMEM((ngrp, gq, 1), F32), pltpu.VMEM((ngrp, gq, 1), F32),
                        pltpu.VMEM((ngrp, gq, LANES), F32)],
        compiler_params=_params("parallel", "arbitrary"),
        name="fox_sample",
    )(z, k_cache, v_cache, k_new, v_new, cq, ck_c, ck_n, pen)


def kernel(x_prompt, x_sample, cache_fox_k, cache_fox_v, cache_fox_logf, state_hgrn, p_prompt, p_sample, ffn1_pre, ffn1_post, ffn1_wg, ffn1_wu, ffn1_wd, mix_pre, mix_post, w_in, fox_fbias, hgrn_lb_logits, hgrn_gnorm, w_branch_a, w_branch_b, w_out, ffn2_pre, ffn2_post, ffn2_wg, ffn2_wu, ffn2_wd, ple_pre, ple_post, w_ple_gate, w_ple_proj):
    depth = w_in.shape[0]
    _, seq, d = x_prompt.shape
    nb, t_new, _ = x_sample.shape
    past = cache_fox_k.shape[2]
    heads, dh = cache_fox_k.shape[3], cache_fox_k.shape[4]
    h_a, dk, dv = state_hgrn.shape[2], state_hgrn.shape[3], state_hgrn.shape[4]
    assert dk == LANES and dv == LANES and dh == LANES and t_new == SUB
    w_ak, w_av, w_b = h_a * dk, h_a * dv, heads * dh
    o_qa, o_fa, o_ia, o_og = 0, w_ak, 2 * w_ak, 2 * w_ak + w_av
    o_qb = 2 * w_ak + 2 * w_av
    o_kb, o_vb = o_qb + w_b, o_qb + 2 * w_b
    o_fl = o_qb + 3 * w_b
    o_ga = o_fl
    o_gb = o_ga + d
    scale = dh ** -0.5
    log2e = float(np.log2(np.e))
    tm = 768
    hgrn_tb = 256
    hgrn_ch = 4
    fox_tq, fox_nsub = 512, 2
    fox_tkb = 512
    cs_tb = 512
    n_new = nb * t_new

    h = jnp.concatenate([x_prompt.reshape(seq, d), x_sample.reshape(n_new, d)], axis=0)
    row = lambda v: v.reshape(1, -1)
    lb_all = jnp.cumsum(jax.nn.softmax(hgrn_lb_logits.astype(F32), axis=0), axis=0)

    new_k, new_v, new_logf, st_p, st_s = [], [], [], [], []
    for l in range(depth):
        a = _ffn_up(h, row(ffn1_pre[l]), ffn1_wg[l].astype(BF16), ffn1_wu[l].astype(BF16), tm, 512)
        h = _mm_post(a, ffn1_wd[l].astype(BF16), h, row(ffn1_post[l]), 0.5, tm, 512)

        w_l = w_in[l]
        w_main = jnp.concatenate([w_l[:, :o_fl], w_l[:, o_fl + heads:]], axis=1).astype(BF16)
        z = _pn_mm(h, row(mix_pre[l]), w_main, tm, 1024)
        w_fl = jnp.pad(w_l[:, o_fl:o_fl + heads], ((0, 0), (0, LANES - heads))).astype(BF16)
        b_fl = jnp.pad(fox_fbias[l], (0, LANES - heads)).reshape(1, LANES)
        logf = _fox_logf(h, row(mix_pre[l]), w_fl, b_fl, tm)

        c_p = _cumsum_rows(logf[:seq].reshape(1, seq, LANES), cs_tb)[0, :, :heads]
        c3 = jnp.stack(_split3(c_p * log2e), axis=-1)
        one3 = jnp.ones_like(c3)
        lane_pad = ((0, 0), (0, 0), (0, LANES - 6))
        caug_qt = jnp.transpose(jnp.pad(jnp.concatenate([c3, one3], axis=-1), lane_pad), (1, 2, 0))
        caug_k = jnp.swapaxes(jnp.pad(jnp.concatenate([one3, -c3], axis=-1), lane_pad), 0, 1)
        logf_new = logf[seq:, :heads].reshape(nb, t_new, heads)
        t_all = past + t_new
        t_pad = -(-t_all // cs_tb) * cs_tb
        lf_all = jnp.concatenate([cache_fox_logf[l].astype(F32), logf_new], axis=1)
        lf_all = jnp.pad(lf_all, ((0, 0), (0, t_pad - t_all), (0, LANES - heads)))
        c_s = _cumsum_rows(lf_all, cs_tb)[:, :t_all, :heads]

        lb = row(lb_all[l])
        gn = row(hgrn_gnorm[l])
        offs_a = (o_qa, o_fa, o_ia, o_og)
        oa_p, s_p = _hgrn(z, offs_a, lb, gn, 0, seq, h_a, hgrn_tb, hgrn_ch, None, seq)
        oa_s, s_s = _hgrn(z, offs_a, lb, gn, seq, n_new, h_a, n_new, 1, state_hgrn[l].astype(F32), n_new)

        ob_p = _fox_prompt(z, (o_qb, o_kb, o_vb), caug_qt, caug_k, seq, heads, fox_tq, fox_nsub,
                           scale * log2e)
        k_new = z[seq:, o_kb:o_kb + w_b].reshape(nb, t_new, heads, dh)
        v_new = z[seq:, o_vb:o_vb + w_b].reshape(nb, t_new, heads, dh)
        ob_s = _fox_sample(z, o_qb, seq, cache_fox_k[l], cache_fox_v[l], k_new, v_new, c_s,
                           t_new, fox_tkb, scale)

        oa = jnp.concatenate([oa_p, oa_s], axis=0)
        ob = jnp.concatenate([ob_p, ob_s], axis=0)
        mg = _mix(oa, ob, z, o_ga, o_gb, w_branch_a[l].astype(BF16), w_branch_b[l].astype(BF16), tm, 512)
        h = _mm_post(mg, w_out[l].astype(BF16), h, row(mix_post[l]), 1.0, tm, 512)

        a = _ffn_up(h, row(ffn2_pre[l]), ffn2_wg[l].astype(BF16), ffn2_wu[l].astype(BF16), tm, 512)
        h = _mm_post(a, ffn2_wd[l].astype(BF16), h, row(ffn2_post[l]), 0.5, tm, 512)

        pe = jnp.concatenate([p_prompt[l].reshape(seq, -1), p_sample[l].reshape(n_new, -1)], axis=0)
        h = _ple(h, row(ple_pre[l]), w_ple_gate[l].astype(BF16), pe, w_ple_proj[l].astype(BF16),
                 row(ple_post[l]), 256)

        new_k.append(z[:, o_kb:o_kb + w_b])
        new_v.append(z[:, o_vb:o_vb + w_b])
        new_logf.append(logf[:, :heads])
        st_p.append(s_p)
        st_s.append(s_s)

    dt_p, dt_s = x_prompt.dtype, x_sample.dtype
    stack = lambda xs: jnp.stack(xs)
    k_all, v_all, lf = stack(new_k), stack(new_v), stack(new_logf)
    return (h[:seq].reshape(x_prompt.shape), h[seq:].reshape(x_sample.shape),
            k_all[:, :seq].reshape(depth, 1, seq, heads, dh).astype(dt_p),
            v_all[:, :seq].reshape(depth, 1, seq, heads, dh).astype(dt_p),
            lf[:, :seq].reshape(depth, 1, seq, heads).astype(dt_p),
            jnp.concatenate(st_p, axis=0).reshape(depth, 1, h_a, dk, dv).astype(dt_p),
            k_all[:, seq:].reshape(depth, nb, t_new, heads, dh).astype(dt_s),
            v_all[:, seq:].reshape(depth, nb, t_new, heads, dh).astype(dt_s),
            lf[:, seq:].reshape(depth, nb, t_new, heads).astype(dt_s),
            jnp.stack(st_s).astype(dt_s))
```

```python
import functools

import numpy as np
import jax
import jax.numpy as jnp
from jax import lax
from jax.experimental import pallas as pl
from jax.experimental.pallas import tpu as pltpu

F32 = jnp.float32
BF16 = jnp.bfloat16
EPS = 1e-6
LANES = 128
SUB = 16
VMEM_LIMIT = 56 * 1024 * 1024


def _params(*sem):
    return pltpu.CompilerParams(dimension_semantics=sem, vmem_limit_bytes=VMEM_LIMIT)


def _sigmoid(x):
    return 1.0 / (1.0 + jnp.exp(-x))


def _silu(x):
    return x * _sigmoid(x)


def _log_sigmoid(x):
    return jnp.minimum(x, 0.0) - jnp.log(1.0 + jnp.exp(-jnp.abs(x)))


def _rms(x):
    return x * lax.rsqrt(jnp.mean(x * x, axis=-1, keepdims=True) + EPS)


def _split3(x):
    x1 = x.astype(BF16)
    r1 = x - x1.astype(F32)
    x2 = r1.astype(BF16)
    x3 = (r1 - x2.astype(F32)).astype(BF16)
    return x1, x2, x3


def _dot3(m, parts):
    acc = jnp.dot(m, parts[0], preferred_element_type=F32)
    acc += jnp.dot(m, parts[1], preferred_element_type=F32)
    acc += jnp.dot(m, parts[2], preferred_element_type=F32)
    return acc


_NT = (((1,), (1,)), ((), ()))
_TN = (((0,), (0,)), ((), ()))


def _prenorm_to_scratch(x_ref, nw_ref, xn_ref):
    @pl.when(pl.program_id(1) == 0)
    def _():
        xn_ref[...] = (_rms(x_ref[...]) * nw_ref[...]).astype(BF16)


def _ffn_up_kernel(x_ref, nw_ref, wg_ref, wu_ref, o_ref, xn_ref):
    _prenorm_to_scratch(x_ref, nw_ref, xn_ref)
    xn = xn_ref[...]
    g = jnp.dot(xn, wg_ref[...], preferred_element_type=F32)
    u = jnp.dot(xn, wu_ref[...], preferred_element_type=F32)
    o_ref[...] = (_silu(g) * u).astype(o_ref.dtype)


def _ffn_up(h, nw, wg, wu, tm, tf):
    m, d = h.shape
    f = wg.shape[1]
    return pl.pallas_call(
        _ffn_up_kernel,
        grid=(m // tm, f // tf),
        in_specs=[pl.BlockSpec((tm, d), lambda i, j: (i, 0)),
                  pl.BlockSpec((1, d), lambda i, j: (0, 0)),
                  pl.BlockSpec((d, tf), lambda i, j: (0, j)),
                  pl.BlockSpec((d, tf), lambda i, j: (0, j))],
        out_specs=pl.BlockSpec((tm, tf), lambda i, j: (i, j)),
        out_shape=jax.ShapeDtypeStruct((m, f), BF16),
        scratch_shapes=[pltpu.VMEM((tm, d), BF16)],
        compiler_params=_params("parallel", "arbitrary"),
        name="ffn_up",
    )(h, nw, wg, wu)


def _mm_post_kernel(a_ref, w_ref, h_ref, pw_ref, o_ref, acc_ref, *, scale, nk):
    k = pl.program_id(1)

    @pl.when(k == 0)
    def _():
        acc_ref[...] = jnp.zeros_like(acc_ref)

    acc_ref[...] += jnp.dot(a_ref[...], w_ref[...], preferred_element_type=F32)

    @pl.when(k == nk - 1)
    def _():
        o_ref[...] = h_ref[...] + scale * (_rms(acc_ref[...]) * pw_ref[...])


def _mm_post(a, w, h, pw, scale, tm, tk):
    m, kdim = a.shape
    d = w.shape[1]
    nk = kdim // tk
    return pl.pallas_call(
        functools.partial(_mm_post_kernel, scale=scale, nk=nk),
        grid=(m // tm, nk),
        in_specs=[pl.BlockSpec((tm, tk), lambda i, k: (i, k)),
                  pl.BlockSpec((tk, d), lambda i, k: (k, 0)),
                  pl.BlockSpec((tm, d), lambda i, k: (i, 0)),
                  pl.BlockSpec((1, d), lambda i, k: (0, 0))],
        out_specs=pl.BlockSpec((tm, d), lambda i, k: (i, 0)),
        out_shape=jax.ShapeDtypeStruct((m, d), F32),
        scratch_shapes=[pltpu.VMEM((tm, d), F32)],
        compiler_params=_params("parallel", "arbitrary"),
        name="mm_post",
    )(a, w, h, pw)


def _pn_mm_kernel(x_ref, nw_ref, w_ref, o_ref, xn_ref):
    _prenorm_to_scratch(x_ref, nw_ref, xn_ref)
    o_ref[...] = jnp.dot(xn_ref[...], w_ref[...], preferred_element_type=F32)


def _pn_mm(h, nw, w, tm, tn):
    m, d = h.shape
    n = w.shape[1]
    return pl.pallas_call(
        _pn_mm_kernel,
        grid=(m // tm, n // tn),
        in_specs=[pl.BlockSpec((tm, d), lambda i, j: (i, 0)),
                  pl.BlockSpec((1, d), lambda i, j: (0, 0)),
                  pl.BlockSpec((d, tn), lambda i, j: (0, j))],
        out_specs=pl.BlockSpec((tm, tn), lambda i, j: (i, j)),
        out_shape=jax.ShapeDtypeStruct((m, n), F32),
        scratch_shapes=[pltpu.VMEM((tm, d), BF16)],
        compiler_params=_params("parallel", "arbitrary"),
        name="w_in_proj",
    )(h, nw, w)


def _fox_logf_kernel(x_ref, nw_ref, w_ref, b_ref, o_ref):
    xn = (_rms(x_ref[...]) * nw_ref[...]).astype(BF16)
    z = jnp.dot(xn, w_ref[...], preferred_element_type=F32)
    o_ref[...] = _log_sigmoid(z + b_ref[...])


def _fox_logf(h, nw, w, bias, tm):
    m, d = h.shape
    n = w.shape[1]
    return pl.pallas_call(
        _fox_logf_kernel,
        grid=(m // tm,),
        in_specs=[pl.BlockSpec((tm, d), lambda i: (i, 0)),
                  pl.BlockSpec((1, d), lambda i: (0, 0)),
                  pl.BlockSpec((d, n), lambda i: (0, 0)),
                  pl.BlockSpec((1, n), lambda i: (0, 0))],
        out_specs=pl.BlockSpec((tm, n), lambda i: (i, 0)),
        out_shape=jax.ShapeDtypeStruct((m, n), F32),
        compiler_params=_params("parallel"),
        name="fox_logf",
    )(h, nw, w, bias)


def _cumsum_kernel(x_ref, tri_ref, o_ref, carry_ref):
    @pl.when(pl.program_id(1) == 0)
    def _():
        carry_ref[...] = jnp.zeros_like(carry_ref)

    c = _dot3(tri_ref[...], _split3(x_ref[0])) + carry_ref[...]
    o_ref[0] = c
    carry_ref[...] = c[-1:, :]


def _cumsum_rows(x, tb):
    b, t, n = x.shape
    r = np.arange(tb)
    tri = jnp.asarray(r[None, :] <= r[:, None], BF16)
    return pl.pallas_call(
        _cumsum_kernel,
        grid=(b, t // tb),
        in_specs=[pl.BlockSpec((1, tb, n), lambda i, j: (i, j, 0)),
                  pl.BlockSpec((tb, tb), lambda i, j: (0, 0))],
        out_specs=pl.BlockSpec((1, tb, n), lambda i, j: (i, j, 0)),
        out_shape=jax.ShapeDtypeStruct((b, t, n), F32),
        scratch_shapes=[pltpu.VMEM((1, n), F32)],
        compiler_params=_params("parallel", "arbitrary"),
        name="cumsum_rows",
    )(x, tri)


def _mix_kernel(oa_ref, ob_ref, ga_ref, gb_ref, wa_ref, wb_ref, o_ref):
    ya = jnp.dot(oa_ref[...], wa_ref[...], preferred_element_type=F32)
    yb = jnp.dot(ob_ref[...], wb_ref[...], preferred_element_type=F32)
    o_ref[...] = (_sigmoid(ga_ref[...]) * ya + _sigmoid(gb_ref[...]) * yb).astype(o_ref.dtype)


def _mix(oa, ob, z, ga_off, gb_off, wa, wb, tm, tn):
    m, d = oa.shape
    n = wa.shape[1]
    ga_blk, gb_blk = ga_off // tn, gb_off // tn
    return pl.pallas_call(
        _mix_kernel,
        grid=(m // tm, n // tn),
        in_specs=[pl.BlockSpec((tm, d), lambda i, j: (i, 0)),
                  pl.BlockSpec((tm, d), lambda i, j: (i, 0)),
                  pl.BlockSpec((tm, tn), lambda i, j: (i, ga_blk + j)),
                  pl.BlockSpec((tm, tn), lambda i, j: (i, gb_blk + j)),
                  pl.BlockSpec((d, tn), lambda i, j: (0, j)),
                  pl.BlockSpec((d, tn), lambda i, j: (0, j))],
        out_specs=pl.BlockSpec((tm, tn), lambda i, j: (i, j)),
        out_shape=jax.ShapeDtypeStruct((m, n), BF16),
        compiler_params=_params("parallel", "arbitrary"),
        name="mix_gate",
    )(oa, ob, z, z, wa, wb)


def _ple_kernel(h_ref, nw_ref, wg_ref, pe_ref, wp_ref, pw_ref, oa_ref, ob_ref, *, na):
    x = h_ref[...]
    xn = (_rms(x) * nw_ref[...]).astype(BF16)
    gate = _sigmoid(jnp.dot(xn, wg_ref[...], preferred_element_type=F32))
    e = jnp.dot(pe_ref[...].astype(BF16), wp_ref[...], preferred_element_type=F32)
    res = x + _rms(gate * e) * pw_ref[...]

    @pl.when(pl.program_id(0) < na)
    def _():
        oa_ref[...] = res

    @pl.when(pl.program_id(0) >= na)
    def _():
        ob_ref[...] = res


def _ple(h, nw, wg, pe, wp, pw, rows_a, tm):
    m, d = h.shape
    p = pe.shape[1]
    na = rows_a // tm
    return pl.pallas_call(
        functools.partial(_ple_kernel, na=na),
        grid=(m // tm,),
        in_specs=[pl.BlockSpec((tm, d), lambda i: (i, 0)),
                  pl.BlockSpec((1, d), lambda i: (0, 0)),
                  pl.BlockSpec((d, d), lambda i: (0, 0)),
                  pl.BlockSpec((tm, p), lambda i: (i, 0)),
                  pl.BlockSpec((p, d), lambda i: (0, 0)),
                  pl.BlockSpec((1, d), lambda i: (0, 0))],
        out_specs=[pl.BlockSpec((tm, d), lambda i: (jnp.minimum(i, na - 1), 0)),
                   pl.BlockSpec((tm, d), lambda i: (jnp.maximum(i - na, 0), 0))],
        out_shape=[jax.ShapeDtypeStruct((rows_a, d), F32),
                   jax.ShapeDtypeStruct((m - rows_a, d), F32)],
        compiler_params=_params("arbitrary"),
        name="ple",
    )(h, nw, wg, pe, wp, pw)


def _hgrn_kernel(*refs, nsb, ch, chain, hps):
    for hh in range(hps):
        lanes = slice(hh * LANES, (hh + 1) * LANES)
        q_ref, f_ref, v_ref, og_ref, lb_ref = [r.at[:, lanes] for r in refs[:5]]
        gn_ref, band_ref, sel_ref = refs[5:8]
        if chain:
            o_ref, s_out_ref, st_ref = refs[-3:]
            tail = (o_ref.at[:, lanes], s_out_ref.at[:, hh:hh + 1], st_ref.at[hh])
        else:
            o_ref, s_out_ref = refs[-2:]
            tail = (refs[8].at[:, hh:hh + 1], o_ref.at[:, lanes], s_out_ref.at[:, hh:hh + 1])
        _hgrn_head(q_ref, f_ref, v_ref, og_ref, lb_ref, gn_ref, band_ref, sel_ref, *tail,
                   nsb=nsb, ch=ch, chain=chain)


def _hgrn_head(*refs, nsb, ch, chain):
    q_ref, f_ref, v_ref, og_ref, lb_ref, gn_ref, band_ref, sel_ref = refs[:8]
    if chain:
        o_ref, s_out_ref, st_ref = refs[-3:]
    else:
        s_in_ref = refs[8]
        o_ref, s_out_ref = refs[-2:]
    tb = nsb * SUB
    cr = ch * SUB
    dk = q_ref.shape[1]

    q = _silu(q_ref[...])
    lb = lb_ref[...]
    f = lb + (1.0 - lb) * _sigmoid(f_ref[...])
    k = 1.0 - f
    t_idx = lax.broadcasted_iota(jnp.int32, (tb, 1), 0)
    t_off = t_idx % SUB
    sub_pos = (t_idx // SUB) % ch

    w = jnp.log(f)
    sh = 1
    while sh < SUB:
        w = w + jnp.where(t_off >= sh, pltpu.roll(w, sh, 0), 0.0)
        sh *= 2
    w3 = w.reshape(nsb, SUB, dk)
    totb = jnp.broadcast_to(w3[:, SUB - 1:SUB, :], (nsb, SUB, dk)).reshape(tb, dk)
    u = totb - w

    e_start = jnp.zeros_like(w)
    f_end = jnp.zeros_like(w)
    between = jnp.zeros_like(w)
    lhs = [(q * jnp.exp(w)).astype(BF16)]
    for dlt in range(1, ch):
        prev = pltpu.roll(totb, dlt * SUB, 0)
        nxt = pltpu.roll(totb, tb - dlt * SUB, 0)
        e_start = e_start + jnp.where(sub_pos >= dlt, prev, 0.0)
        f_end = f_end + jnp.where(sub_pos < ch - dlt, nxt, 0.0)
        between = between + prev
        if dlt + 1 < ch:
            lhs.append((q * jnp.exp(w + between)).astype(BF16))
    kh = (k * jnp.exp(u)).astype(BF16)
    vb = v_ref[...].astype(BF16)
    if ch > 1:
        qs = (q * jnp.exp(w + e_start)).astype(BF16)
        ke = (k * jnp.exp(u + f_end)).astype(BF16)
    else:
        qs, ke = lhs[0], kh
    ctot = e_start + totb + f_end

    nchunk = nsb // ch
    upd = [lax.dot_general(vb[c * cr:(c + 1) * cr], ke[c * cr:(c + 1) * cr], _TN,
                           preferred_element_type=F32) for c in range(nchunk)]

    half = SUB // 2
    q3 = q.reshape(nsb, SUB, dk)
    wk3 = (w - jnp.log(k)).reshape(nsb, SUB, dk)
    pieces = []
    for s in range(SUB):
        lo = 0 if s < half else half
        p = q3[:, lo:, :] * jnp.exp(jnp.minimum(w3[:, lo:, :] - wk3[:, s:s + 1, :], 0.0))
        if lo:
            p = jnp.concatenate([jnp.zeros((nsb, lo, dk), F32), p], axis=1)
        pieces.append(p.reshape(tb, dk).astype(BF16))
    band = band_ref[...]
    a = jnp.dot(jnp.concatenate(pieces, axis=1), sel_ref[...], preferred_element_type=F32)
    a = jnp.where(band == 1.0, a, 0.0)
    if ch > 1:
        r = lax.dot_general(jnp.concatenate(lhs, axis=0), kh, _NT, preferred_element_type=F32)
        for dlt in range(1, ch):
            a = jnp.where(band == dlt + 1.0, r[(dlt - 1) * tb:dlt * tb], a)
    o_in = jnp.dot(a.astype(BF16), vb, preferred_element_type=F32)

    if chain:
        st = jnp.where(pl.program_id(1) == 0, 0.0, st_ref[...])
    outs = []
    for c in range(nchunk):
        rows = slice(c * cr, (c + 1) * cr)
        if not chain:
            st = s_in_ref[c, 0].T
        outs.append(o_in[rows] + lax.dot_general(qs[rows], st.astype(BF16), _NT,
                                                 preferred_element_type=F32))
        st = st * jnp.exp(ctot[c * cr:c * cr + 1, :]) + upd[c]
        if not chain:
            s_out_ref[c, 0] = st.T
    if chain:
        st_ref[...] = st
        s_out_ref[0, 0] = st.T

    o = jnp.concatenate(outs, axis=0)
    o_ref[...] = (_rms(o) * gn_ref[...] * _silu(og_ref[...])).astype(o_ref.dtype)


def _hgrn_consts(tb, ch):
    t = np.arange(tb)
    blk = t // SUB
    d = blk[:, None] - blk[None, :]
    same_chunk = (blk[:, None] // ch) == (blk[None, :] // ch)
    band = np.where((d == 0) & (t[None, :] <= t[:, None]), 1, 0)
    band = np.where(same_chunk & (d >= 1), d + 1, band)
    sel = np.repeat(np.arange(SUB), LANES)[:, None] == (t[None, :] % SUB)
    return jnp.asarray(band, F32), jnp.asarray(sel, BF16)


def _hgrn(z, offs, lb, gnorm, row0, rows, heads, tb, ch, hps, s_in, into):
    hw = hps * LANES
    q_off, f_off, v_off, og_off = [o // hw for o in offs]
    rb0 = row0 // tb
    nt = rows // tb
    nsb = tb // SUB
    band, sel = _hgrn_consts(tb, ch)
    chain = s_in is None

    def col(off):
        return pl.BlockSpec((tb, hw), lambda h, t: (rb0 + t, off + h))

    const2 = lambda h, t: (0, 0)
    in_specs = [col(q_off), col(f_off), col(v_off), col(og_off),
                pl.BlockSpec((1, hw), lambda h, t: (0, h)),
                pl.BlockSpec((1, LANES), const2),
                pl.BlockSpec((tb, tb), const2),
                pl.BlockSpec((SUB * LANES, tb), const2)]
    args = [z, z, z, z, lb, gnorm, band, sel]
    scratch = []
    if chain:
        s_shape = (1, heads, LANES, LANES)
        s_spec = pl.BlockSpec((1, hps, LANES, LANES), lambda h, t: (0, h, 0, 0))
        scratch = [pltpu.VMEM((hps, LANES, LANES), F32)]
    else:
        s_shape = s_in.shape
        s_spec = pl.BlockSpec((nsb // ch, hps, LANES, LANES), lambda h, t: (t, h, 0, 0))
        in_specs.append(s_spec)
        args.append(s_in)
    aliases = {}
    if into is not None:
        aliases = {len(args): 0}
        in_specs.append(pl.BlockSpec(memory_space=pl.ANY))
        args.append(into)
    return pl.pallas_call(
        functools.partial(_hgrn_kernel, nsb=nsb, ch=ch, chain=chain, hps=hps),
        grid=(heads // hps, nt),
        in_specs=in_specs,
        out_specs=[pl.BlockSpec((tb, hw), lambda h, t: (rb0 + t, h)), s_spec],
        out_shape=[jax.ShapeDtypeStruct((z.shape[0], heads * LANES), BF16),
                   jax.ShapeDtypeStruct(s_shape, F32)],
        scratch_shapes=scratch,
        input_output_aliases=aliases,
        compiler_params=_params("parallel", "arbitrary"),
        name="hgrn_chain" if chain else "hgrn_step",
    )(*args)


def _fold_rows(x, op):
    while x.shape[0] % 16 == 0 and x.shape[0] > 64:
        half = x.shape[0] // 2
        x = op(x[:half], x[half:])
    red = jnp.max if op is jnp.maximum else jnp.sum
    return red(x, axis=0, keepdims=True)


def _softmax_step_t(st, cols, vt, m_ref, l_ref, acc_ref):
    m_old = m_ref[:, cols]
    m_new = jnp.maximum(m_old, _fold_rows(st, jnp.maximum))
    alpha = jnp.exp2(m_old - m_new)
    p = jnp.exp2(st - m_new)
    l_ref[:, cols] = alpha * l_ref[:, cols] + _fold_rows(p, jnp.add)
    acc_ref[:, cols] = alpha * acc_ref[:, cols] + jnp.dot(vt, p.astype(BF16),
                                                          preferred_element_type=F32)
    m_ref[:, cols] = m_new


def _cterm_rows(c_row, key_side):
    parts = [p.astype(F32) for p in _split3(c_row)]
    r = lax.broadcasted_iota(jnp.int32, (LANES, c_row.shape[1]), 0)
    base = 3 if key_side else 0
    out = jnp.where((r >= 3 - base) & (r < 6 - base), 1.0, 0.0)
    for i, p in enumerate(parts):
        out = jnp.where(r == base + i, -p if key_side else p, out)
    return out


def _fox_prompt_kernel(q_ref, cq_ref, k_ref, ck_ref, v_ref, o_ref,
                       qa_ref, ka_ref, vt_ref, m_ref, l_ref, acc_ref, s_ref, *, scale2, tq, nsub):
    qi = pl.program_id(1)
    nkb = ka_ref.shape[0]
    sr = tq // nsub

    @pl.when(qi == 0)
    def _():
        for c in range(nkb):
            rows = slice(c * tq, (c + 1) * tq)
            ka_ref[c, :, :LANES] = k_ref[rows, :].astype(BF16)
            ka_ref[c, :, LANES:] = _cterm_rows(ck_ref[0, :, rows], key_side=True).T.astype(BF16)
            vt_ref[c] = v_ref[rows, :].T.astype(BF16)

    qa_ref[:LANES, :] = (q_ref[...] * scale2).T.astype(BF16)
    qa_ref[LANES:, :] = _cterm_rows(cq_ref[0], key_side=False).astype(BF16)
    m_ref[...] = jnp.full_like(m_ref, -jnp.inf)
    l_ref[...] = jnp.zeros_like(l_ref)
    acc_ref[...] = jnp.zeros_like(acc_ref)

    def scores(j, slot):
        s_ref[slot] = jnp.dot(ka_ref[j], qa_ref[...], preferred_element_type=F32)

    def softmax_pv(j, slot, diagonal):
        for r in range(nsub):
            cols = slice(r * sr, (r + 1) * sr)
            n = (r + 1) * sr if diagonal else tq
            st = s_ref[slot, :n, cols]
            if diagonal:
                key = lax.broadcasted_iota(jnp.int32, st.shape, 0)
                qry = lax.broadcasted_iota(jnp.int32, st.shape, 1) + r * sr
                st = jnp.where(key <= qry, st, -jnp.inf)
            _softmax_step_t(st, cols, vt_ref[j, :, :n], m_ref, l_ref, acc_ref)

    scores(0, 0)

    def two_blocks(jj, carry):
        j = 2 * jj
        scores(j + 1, 1)
        softmax_pv(j, 0, False)
        scores(j + 2, 0)
        softmax_pv(j + 1, 1, False)
        return carry

    lax.fori_loop(0, qi // 2, two_blocks, 0)

    @pl.when(qi % 2 == 0)
    def _():
        softmax_pv(qi, 0, True)

    @pl.when(qi % 2 == 1)
    def _():
        scores(qi, 1)
        softmax_pv(qi - 1, 0, False)
        softmax_pv(qi, 1, True)

    o_ref[...] = (acc_ref[...] / l_ref[...]).T.astype(o_ref.dtype)


def _fox_prompt(z, offs, c2, rows, heads, tq, nsub, scale2):
    q_off, k_off, v_off = [o // LANES for o in offs]
    nkb = rows // tq
    return pl.pallas_call(
        functools.partial(_fox_prompt_kernel, scale2=scale2, tq=tq, nsub=nsub),
        grid=(heads, nkb),
        in_specs=[pl.BlockSpec((tq, LANES), lambda h, i: (i, q_off + h)),
                  pl.BlockSpec((1, 1, tq), lambda h, i: (h, 0, i)),
                  pl.BlockSpec((rows, LANES), lambda h, i: (0, k_off + h)),
                  pl.BlockSpec((1, 1, rows), lambda h, i: (h, 0, 0)),
                  pl.BlockSpec((rows, LANES), lambda h, i: (0, v_off + h))],
        out_specs=pl.BlockSpec((tq, LANES), lambda h, i: (i, h)),
        out_shape=jax.ShapeDtypeStruct((z.shape[0], heads * LANES), BF16),
        scratch_shapes=[pltpu.VMEM((2 * LANES, tq), BF16),
                        pltpu.VMEM((nkb, tq, 2 * LANES), BF16),
                        pltpu.VMEM((nkb, LANES, tq), BF16),
                        pltpu.VMEM((1, tq), F32), pltpu.VMEM((1, tq), F32),
                        pltpu.VMEM((LANES, tq), F32),
                        pltpu.VMEM((2, tq, tq), F32)],
        compiler_params=_params("parallel", "arbitrary"),
        name="fox_prompt",
    )(z, c2, z, c2, z)


HEAD_GROUP = 8


def _fox_sample_kernel(q_ref, kc_ref, vc_ref, kn_ref, vn_ref, cq_ref, ckc_ref, ckn_ref, pen_ref,
                       into_ref, o_ref, q8_ref, m_ref, l_ref, acc_ref, *, scale, t_new):
    j = pl.program_id(1)
    heads = kc_ref.shape[1]
    ngrp = heads // HEAD_GROUP
    gq = HEAD_GROUP * t_new

    @pl.when(j == 0)
    def _():
        for h in range(heads):
            q8_ref[h // HEAD_GROUP, (h % HEAD_GROUP) * t_new:(h % HEAD_GROUP + 1) * t_new, :] = (
                q_ref[:, h * LANES:(h + 1) * LANES] * scale).astype(BF16)
        m_ref[...] = jnp.full_like(m_ref, -jnp.inf)
        l_ref[...] = jnp.zeros_like(l_ref)
        acc_ref[...] = jnp.zeros_like(acc_ref)

    def update(g, s, vv):
        m_old = m_ref[g]
        m_new = jnp.maximum(m_old, jnp.max(s, axis=-1, keepdims=True))
        alpha = jnp.exp(m_old - m_new)
        p = jnp.exp(s - m_new)
        l_ref[g] = alpha * l_ref[g] + jnp.sum(p, axis=-1, keepdims=True)
        acc_ref[g] = alpha * acc_ref[g] + jnp.dot(p.astype(BF16), vv, preferred_element_type=F32)
        m_ref[g] = m_new

    def group_rows(ref, g):
        x = ref[:, g * HEAD_GROUP:(g + 1) * HEAD_GROUP, :]
        return x.reshape(x.shape[0] * HEAD_GROUP, x.shape[2]).astype(BF16)

    for g in range(ngrp):
        s = lax.dot_general(q8_ref[g], group_rows(kc_ref, g), _NT, preferred_element_type=F32)
        update(g, s + pen_ref[...] + (cq_ref[0, g] - ckc_ref[0, g]), group_rows(vc_ref, g))

    @pl.when(j == pl.num_programs(1) - 1)
    def _():
        for g in range(ngrp):
            s = lax.dot_general(q8_ref[g], group_rows(kn_ref, g), _NT, preferred_element_type=F32)
            s = s + pen_ref[:, :gq] + (cq_ref[0, g] - ckn_ref[0, g])
            qry = lax.broadcasted_iota(jnp.int32, s.shape, 0) % t_new
            key = lax.broadcasted_iota(jnp.int32, s.shape, 1) // HEAD_GROUP
            update(g, jnp.where(key <= qry, s, -jnp.inf), group_rows(vn_ref, g))
            o = acc_ref[g] / l_ref[g]
            for hl in range(HEAD_GROUP):
                h = g * HEAD_GROUP + hl
                o_ref[:, h * LANES:(h + 1) * LANES] = o[hl * t_new:(hl + 1) * t_new].astype(o_ref.dtype)


def _fox_sample(z, q_off, row0, k_cache, v_cache, k_new, v_new, c_all, t_new, tkb, scale, into):
    nb, past, heads, _ = k_cache.shape
    ngrp = heads // HEAD_GROUP
    gq = HEAD_GROUP * t_new
    w_b = heads * LANES
    rb0 = row0 // t_new
    grp = lambda c: jnp.swapaxes(c.reshape(nb, c.shape[1], ngrp, HEAD_GROUP), 1, 2)
    ck_c = grp(c_all[:, :past]).reshape(nb, ngrp, 1, past * HEAD_GROUP)
    c_n = grp(c_all[:, past:])
    ck_n = c_n.reshape(nb, ngrp, 1, gq)
    cq = jnp.swapaxes(c_n, 2, 3).reshape(nb, ngrp, gq, 1)
    r = np.arange(gq)[:, None] // t_new
    c = np.arange(tkb * HEAD_GROUP)[None, :] % HEAD_GROUP
    pen = jnp.asarray(np.where(r == c, 0.0, -1e30), F32)
    cache_spec = pl.BlockSpec((None, tkb, heads, LANES), lambda b, j: (b, j, 0, 0))
    new_spec = pl.BlockSpec((None, t_new, heads, LANES), lambda b, j: (b, 0, 0, 0))
    return pl.pallas_call(
        functools.partial(_fox_sample_kernel, scale=scale, t_new=t_new),
        grid=(nb, past // tkb),
        in_specs=[pl.BlockSpec((t_new, w_b), lambda b, j: (rb0 + b, q_off // w_b)),
                  cache_spec, cache_spec, new_spec, new_spec,
                  pl.BlockSpec((1, ngrp, gq, 1), lambda b, j: (b, 0, 0, 0)),
                  pl.BlockSpec((1, ngrp, 1, tkb * HEAD_GROUP), lambda b, j: (b, 0, 0, j)),
                  pl.BlockSpec((1, ngrp, 1, gq), lambda b, j: (b, 0, 0, 0)),
                  pl.BlockSpec((gq, tkb * HEAD_GROUP), lambda b, j: (0, 0)),
                  pl.BlockSpec(memory_space=pl.ANY)],
        out_specs=pl.BlockSpec((t_new, w_b), lambda b, j: (rb0 + b, 0)),
        out_shape=jax.ShapeDtypeStruct(into.shape, BF16),
        scratch_shapes=[pltpu.VMEM((ngrp, gq, LANES), BF16),
                        pltpu.VMEM((ngrp, gq, 1), F32), pltpu.VMEM((ngrp, gq, 1), F32),
                        pltpu.VMEM((ngrp, gq, LANES), F32)],
        input_output_aliases={9: 0},
        compiler_params=_params("parallel", "arbitrary"),
        name="fox_sample",
    )(z, k_cache, v_cache, k_new, v_new, cq, ck_c, ck_n, pen, into)


def kernel(x_prompt, x_sample, cache_fox_k, cache_fox_v, cache_fox_logf, state_hgrn, p_prompt, p_sample, ffn1_pre, ffn1_post, ffn1_wg, ffn1_wu, ffn1_wd, mix_pre, mix_post, w_in, fox_fbias, hgrn_lb_logits, hgrn_gnorm, w_branch_a, w_branch_b, w_out, ffn2_pre, ffn2_post, ffn2_wg, ffn2_wu, ffn2_wd, ple_pre, ple_post, w_ple_gate, w_ple_proj):
    depth = w_in.shape[0]
    _, seq, d = x_prompt.shape
    nb, t_new, _ = x_sample.shape
    past = cache_fox_k.shape[2]
    heads, dh = cache_fox_k.shape[3], cache_fox_k.shape[4]
    h_a, dk, dv = state_hgrn.shape[2], state_hgrn.shape[3], state_hgrn.shape[4]
    assert dk == LANES and dv == LANES and dh == LANES and t_new == SUB
    w_ak, w_av, w_b = h_a * dk, h_a * dv, heads * dh
    o_qa, o_fa, o_ia, o_og = 0, w_ak, 2 * w_ak, 2 * w_ak + w_av
    o_qb = 2 * w_ak + 2 * w_av
    o_kb, o_vb = o_qb + w_b, o_qb + 2 * w_b
    o_fl = o_qb + 3 * w_b
    o_ga = o_fl
    o_gb = o_ga + d
    scale = dh ** -0.5
    log2e = float(np.log2(np.e))
    tm = 768
    hgrn_tb = 256
    hgrn_ch = 4
    hgrn_hps = 4
    fox_tq, fox_nsub = 512, 2
    fox_tkb = 512
    cs_tb = 512
    n_new = nb * t_new

    h = jnp.concatenate([x_prompt.reshape(seq, d), x_sample.reshape(n_new, d)], axis=0)
    row = lambda v: v.reshape(1, -1)
    lb_all = jnp.cumsum(jax.nn.softmax(hgrn_lb_logits.astype(F32), axis=0), axis=0)

    new_k, new_v, new_logf, st_p, st_s = [], [], [], [], []
    for l in range(depth):
        a = _ffn_up(h, row(ffn1_pre[l]), ffn1_wg[l].astype(BF16), ffn1_wu[l].astype(BF16), tm, 512)
        h = _mm_post(a, ffn1_wd[l].astype(BF16), h, row(ffn1_post[l]), 0.5, tm, 512)

        w_l = w_in[l]
        w_main = jnp.concatenate([w_l[:, :o_fl], w_l[:, o_fl + heads:]], axis=1).astype(BF16)
        z = _pn_mm(h, row(mix_pre[l]), w_main, tm, 1024)
        w_fl = jnp.pad(w_l[:, o_fl:o_fl + heads], ((0, 0), (0, LANES - heads))).astype(BF16)
        b_fl = jnp.pad(fox_fbias[l], (0, LANES - heads)).reshape(1, LANES)
        logf = _fox_logf(h, row(mix_pre[l]), w_fl, b_fl, tm)

        c_p = _cumsum_rows(logf[:seq].reshape(1, seq, LANES), cs_tb)[0, :, :heads]
        c2_p = (c_p.T * log2e).reshape(heads, 1, seq)
        logf_new = logf[seq:, :heads].reshape(nb, t_new, heads)
        t_all = past + t_new
        t_pad = -(-t_all // cs_tb) * cs_tb
        lf_all = jnp.concatenate([cache_fox_logf[l].astype(F32), logf_new], axis=1)
        lf_all = jnp.swapaxes(lf_all, 0, 1).reshape(1, t_all, nb * heads)
        lf_all = jnp.pad(lf_all, ((0, 0), (0, t_pad - t_all), (0, 0)))
        c_s = _cumsum_rows(lf_all, cs_tb)[0, :t_all].reshape(t_all, nb, heads)
        c_s = jnp.swapaxes(c_s, 0, 1)

        lb = row(lb_all[l])
        gn = row(hgrn_gnorm[l])
        offs_a = (o_qa, o_fa, o_ia, o_og)
        oa, s_p = _hgrn(z, offs_a, lb, gn, 0, seq, h_a, hgrn_tb, hgrn_ch, hgrn_hps, None, None)
        oa, s_s = _hgrn(z, offs_a, lb, gn, seq, n_new, h_a, n_new, 1, 1, state_hgrn[l].astype(F32), oa)

        ob = _fox_prompt(z, (o_qb, o_kb, o_vb), c2_p, seq, heads, fox_tq, fox_nsub, scale * log2e)
        k_new = z[seq:, o_kb:o_kb + w_b].reshape(nb, t_new, heads, dh)
        v_new = z[seq:, o_vb:o_vb + w_b].reshape(nb, t_new, heads, dh)
        ob = _fox_sample(z, o_qb, seq, cache_fox_k[l], cache_fox_v[l], k_new, v_new, c_s,
                         t_new, fox_tkb, scale, ob)

        mg = _mix(oa, ob, z, o_ga, o_gb, w_branch_a[l].astype(BF16), w_branch_b[l].astype(BF16), tm, 512)
        h = _mm_post(mg, w_out[l].astype(BF16), h, row(mix_post[l]), 1.0, tm, 512)

        a = _ffn_up(h, row(ffn2_pre[l]), ffn2_wg[l].astype(BF16), ffn2_wu[l].astype(BF16), tm, 512)
        h = _mm_post(a, ffn2_wd[l].astype(BF16), h, row(ffn2_post[l]), 0.5, tm, 512)

        pe = jnp.concatenate([p_prompt[l].reshape(seq, -1), p_sample[l].reshape(n_new, -1)], axis=0)
        h_p, h_s = _ple(h, row(ple_pre[l]), w_ple_gate[l].astype(BF16), pe, w_ple_proj[l].astype(BF16),
                        row(ple_post[l]), seq, 256)
        if l + 1 < depth:
            h = jnp.concatenate([h_p, h_s], axis=0)

        new_k.append(z[:, o_kb:o_kb + w_b])
        new_v.append(z[:, o_vb:o_vb + w_b])
        new_logf.append(logf[:, :heads])
        st_p.append(s_p)
        st_s.append(s_s)

    dt_p, dt_s = x_prompt.dtype, x_sample.dtype
    stack = lambda xs: jnp.stack(xs)
    k_all, v_all, lf = stack(new_k), stack(new_v), stack(new_logf)
    return (h_p.reshape(x_prompt.shape), h_s.reshape(x_sample.shape),
            k_all[:, :seq].reshape(depth, 1, seq, heads, dh).astype(dt_p),
            v_all[:, :seq].reshape(depth, 1, seq, heads, dh).astype(dt_p),
            lf[:, :seq].reshape(depth, 1, seq, heads).astype(dt_p),
            jnp.concatenate(st_p, axis=0).reshape(depth, 1, h_a, dk, dv).astype(dt_p),
            k_all[:, seq:].reshape(depth, nb, t_new, heads, dh).astype(dt_s),
            v_all[:, seq:].reshape(depth, nb, t_new, heads, dh).astype(dt_s),
            lf[:, seq:].reshape(depth, nb, t_new, heads).astype(dt_s),
            jnp.stack(st_s).astype(dt_s))
```
